```python
import math
import jax
import jax.numpy as jnp
from jax import lax
import numpy as np


D_MODEL = 2048
BATCH = 4
SEQ = 8192
DEPTH = 1

N_META = 16
CHUNK = 128
LEAD = (-N_META) % CHUNK
NORM_EPS = 1e-6

RET_HEADS = 8
RET_QK_HEAD = D_MODEL // RET_HEADS
RET_V_HEAD = 2 * RET_QK_HEAD
RET_QK = RET_HEADS * RET_QK_HEAD
RET_V = RET_HEADS * RET_V_HEAD
ROPE_BASE = 10000.0

SSM_INNER = 2 * D_MODEL
SSM_HEAD_DIM = 64
SSM_HEADS = SSM_INNER // SSM_HEAD_DIM
SSM_GROUPS = 8
SSM_STATE = 128
CONV_WIDTH = 4
CONV_DIM = SSM_INNER + 2 * SSM_GROUPS * SSM_STATE

PEER_HEADS = 8
PEER_NKEYS = 128
PEER_EXPERTS = PEER_NKEYS * PEER_NKEYS
PEER_TOPK = 16
PEER_QUERY = 256
PEER_HALF = PEER_QUERY // 2
PEER_BLOCK = 128

IN_SIZES = (RET_QK, RET_QK, RET_V, RET_V, SSM_INNER, CONV_DIM, SSM_HEADS, D_MODEL, D_MODEL)
N_PROJ = 2 * RET_QK + 2 * RET_V + SSM_INNER + CONV_DIM + SSM_HEADS + 2 * D_MODEL

kernel_name = "hybrid_retention_ssd_peer_meta"


def _rms(x):
    xf = x.astype(jnp.float32)
    return xf * lax.rsqrt(jnp.mean(xf * xf, axis=-1, keepdims=True) + NORM_EPS)


def rmsnorm(x, g):
    return (_rms(x) * g.astype(jnp.float32)).astype(x.dtype)


def pad_front(t):
    return jnp.pad(t, ((0, 0), (LEAD, 0)) + ((0, 0),) * (t.ndim - 2))


def to_chunks(t):
    b, lp = t.shape[:2]
    return jnp.swapaxes(t.reshape((b, lp // CHUNK, CHUNK) + t.shape[2:]), 0, 1)


def from_chunks(t):
    t = jnp.swapaxes(t, 0, 1)
    return t.reshape((t.shape[0], t.shape[1] * t.shape[2]) + t.shape[3:])


def rotary(t, pos):
    half = t.shape[-1] // 2
    inv = ROPE_BASE ** (-jnp.arange(half, dtype=jnp.float32) / half)
    ang = pos[:, None] * inv[None, :]
    cos = jnp.cos(ang)[None, :, None, :]
    sin = jnp.sin(ang)[None, :, None, :]
    tf = t.astype(jnp.float32)
    t1 = tf[..., 0::2]
    t2 = tf[..., 1::2]
    return jnp.stack([t1 * cos - t2 * sin, t1 * sin + t2 * cos], axis=-1).reshape(t.shape)


def retention_chunked(q, k, v, log_gamma):
    q, k, v = (a.astype(jnp.float32) for a in (q, k, v))
    b, lp, nh, dk = q.shape
    dv = v.shape[-1]
    idx = jnp.arange(CHUNK, dtype=jnp.float32)
    rel = idx[:, None] - idx[None, :]
    dmask = jnp.where(rel >= 0, jnp.exp(log_gamma[:, None, None] * jnp.maximum(rel, 0.0)), 0.0)
    xi = jnp.exp(log_gamma[None, :] * (idx[:, None] + 1.0))
    zeta = jnp.exp(log_gamma[None, :] * (CHUNK - 1.0 - idx[:, None]))
    chunk_decay = jnp.exp(log_gamma * CHUNK)

    def step(state, inp):
        qc, kc, vc = inp
        s = jnp.einsum('bihd,bjhd->bhij', qc, kc) * dmask[None]
        inner = jnp.einsum('bhij,bjhe->bihe', s, vc)
        cross = jnp.einsum('bihd,bhde->bihe', qc, state) * xi[None, :, :, None]
        state = state * chunk_decay[None, :, None, None] + jnp.einsum(
            'bjhd,bjhe->bhde', kc * zeta[None, :, :, None], vc)
        return state, inner + cross

    state0 = jnp.zeros((b, nh, dk, dv), jnp.float32)
    _, ys = lax.scan(step, state0, (to_chunks(q), to_chunks(k), to_chunks(v)))
    return from_chunks(ys)


def ssd_chunked(xdt, a, bm, cm):
    b, lp, nh, p = xdt.shape
    g = bm.shape[2]
    n = bm.shape[3]
    hg = nh // g
    xdt = xdt.astype(jnp.float32).reshape(b, lp, g, hg, p)
    a = a.astype(jnp.float32).reshape(b, lp, g, hg)
    causal = jnp.tril(jnp.ones((CHUNK, CHUNK), dtype=bool))

    def step(state, inp):
        xc, ac, bc, cc = inp
        acs = jnp.cumsum(ac, axis=1)
        diff = acs[:, :, None] - acs[:, None, :]
        lmat = jnp.exp(jnp.where(causal[None, :, :, None, None], diff, -jnp.inf))
        cb = jnp.einsum('bign,bjgn->bijg', cc, bc)
        y_diag = jnp.einsum('bijg,bijgh,bjghp->bighp', cb, lmat, xc)
        y_off = jnp.einsum('bign,bghpn->bighp', cc, state) * jnp.exp(acs)[..., None]
        dec = jnp.exp(acs[:, -1:] - acs)
        state = state * jnp.exp(acs[:, -1])[..., None, None] + jnp.einsum(
            'bjgn,bjgh,bjghp->bghpn', bc, dec, xc)
        return state, y_diag + y_off

    state0 = jnp.zeros((b, g, hg, p, n), jnp.float32)
    _, ys = lax.scan(step, state0, (to_chunks(xdt), to_chunks(a),
                                    to_chunks(bm.astype(jnp.float32)), to_chunks(cm.astype(jnp.float32))))
    return from_chunks(ys).reshape(b, lp, nh, p)


def causal_dwconv(x, w, bias):
    y = lax.conv_general_dilated(
        x, w[:, None, :].astype(x.dtype), window_strides=(1,),
        padding=((CONV_WIDTH - 1, 0),), dimension_numbers=('NWC', 'WIO', 'NWC'),
        feature_group_count=x.shape[-1])
    return y + bias.astype(x.dtype)


def peer_ffn(xn, w_q, sub_keys, exp_u, exp_v):
    b, seq_len, d = xn.shape
    n_tok = b * seq_len
    n_blk = -(-n_tok // PEER_BLOCK)
    xt = jnp.pad(xn.reshape(n_tok, d), ((0, n_blk * PEER_BLOCK - n_tok), (0, 0)))
    xt = xt.reshape(n_blk, PEER_BLOCK, d)

    def block(xb):
        q = (xb @ w_q).reshape(PEER_BLOCK, PEER_HEADS, 2, PEER_HALF)
        s = jnp.einsum('thcd,hckd->thck', q, sub_keys).astype(jnp.float32)
        s1, i1 = lax.top_k(s[:, :, 0], PEER_TOPK)
        s2, i2 = lax.top_k(s[:, :, 1], PEER_TOPK)
        cand = (s1[..., :, None] + s2[..., None, :]).reshape(PEER_BLOCK, PEER_HEADS, PEER_TOPK * PEER_TOPK)
        cand_id = (i1[..., :, None] * PEER_NKEYS + i2[..., None, :]).reshape(
            PEER_BLOCK, PEER_HEADS, PEER_TOPK * PEER_TOPK)
        top_s, top_pos = lax.top_k(cand, PEER_TOPK)
        eid = jnp.take_along_axis(cand_id, top_pos, axis=-1)
        gate = jax.nn.softmax(top_s, axis=-1)
        act = jax.nn.gelu(jnp.einsum('td,thkd->thk', xb, exp_u[eid]).astype(jnp.float32), approximate=False)
        return jnp.einsum('thk,thkd->td', (gate * act).astype(xb.dtype), exp_v[eid])

    y = lax.map(block, xt)
    return y.reshape(n_blk * PEER_BLOCK, d)[:n_tok].reshape(b, seq_len, d)


def setup_inputs(seed: int = 0) -> dict:
    key = jax.random.key(seed)
    ks = jax.random.split(key, 20)
    f32 = jnp.float32

    def nrm(k, shape, scale):
        return scale * jax.random.normal(k, shape, f32)

    x = nrm(ks[0], (BATCH, SEQ, D_MODEL), 1.0)
    meta_tokens = nrm(ks[1], (N_META, D_MODEL), 1.0)
    norm_mix_g = 1.0 + nrm(ks[2], (DEPTH, D_MODEL), 0.01)
    w_in = nrm(ks[3], (DEPTH, D_MODEL, N_PROJ), D_MODEL ** -0.5)
    conv_w = nrm(ks[4], (DEPTH, CONV_WIDTH, CONV_DIM), CONV_WIDTH ** -0.5)
    conv_b = nrm(ks[5], (DEPTH, CONV_DIM), 0.01)
    dt0 = jnp.exp(jax.random.uniform(ks[6], (DEPTH, SSM_HEADS), f32, math.log(1e-3), math.log(1e-1)))
    dt_bias = dt0 + jnp.log(-jnp.expm1(-dt0))
    a_log = jnp.log(jax.random.uniform(ks[7], (DEPTH, SSM_HEADS), f32, 1.0, 16.0))
    d_skip = 1.0 + nrm(ks[8], (DEPTH, SSM_HEADS), 0.01)
    ssm_norm_g = 1.0 + nrm(ks[9], (DEPTH, SSM_INNER), 0.01)
    w_ret_o = nrm(ks[10], (DEPTH, RET_V, D_MODEL), RET_V ** -0.5)
    w_ssm_o = nrm(ks[11], (DEPTH, SSM_INNER, D_MODEL), SSM_INNER ** -0.5)
    w_out = nrm(ks[12], (DEPTH, D_MODEL, D_MODEL), D_MODEL ** -0.5)
    norm_ffn_g = 1.0 + nrm(ks[13], (DEPTH, D_MODEL), 0.01)
    peer_w_q = nrm(ks[14], (DEPTH, D_MODEL, PEER_HEADS * PEER_QUERY), D_MODEL ** -0.5)
    peer_sub_keys = nrm(ks[15], (DEPTH, PEER_HEADS, 2, PEER_NKEYS, PEER_HALF), PEER_HALF ** -0.5)
    peer_u = nrm(ks[16], (DEPTH, PEER_EXPERTS, D_MODEL), D_MODEL ** -0.5)
    peer_v = nrm(ks[17], (DEPTH, PEER_EXPERTS, D_MODEL), PEER_HEADS ** -0.5)
    norm_final_g = 1.0 + nrm(ks[18], (D_MODEL,), 0.01)
    return {"x": x, "meta_tokens": meta_tokens, "norm_mix_g": norm_mix_g, "w_in": w_in,
            "conv_w": conv_w, "conv_b": conv_b, "dt_bias": dt_bias, "a_log": a_log,
            "d_skip": d_skip, "ssm_norm_g": ssm_norm_g, "w_ret_o": w_ret_o, "w_ssm_o": w_ssm_o,
            "w_out": w_out, "norm_ffn_g": norm_ffn_g, "peer_w_q": peer_w_q,
            "peer_sub_keys": peer_sub_keys, "peer_u": peer_u, "peer_v": peer_v,
            "norm_final_g": norm_final_g}


def reference(x, meta_tokens, norm_mix_g, w_in, conv_w, conv_b, dt_bias, a_log, d_skip,
              ssm_norm_g, w_ret_o, w_ssm_o, w_out, norm_ffn_g, peer_w_q, peer_sub_keys,
              peer_u, peer_v, norm_final_g):
    f32 = jnp.float32
    b = x.shape[0]
    h = jnp.concatenate(
        [jnp.broadcast_to(meta_tokens.astype(x.dtype)[None], (b, N_META, D_MODEL)), x], axis=1)
    seq_len = h.shape[1]
    pos = jnp.arange(seq_len, dtype=f32)
    log_gamma = jnp.log(1.0 - 2.0 ** (-5.0 - jnp.arange(RET_HEADS, dtype=f32)))
    split_points = np.cumsum(np.array(IN_SIZES))[:-1].tolist()

    for l in range(DEPTH):
        n = rmsnorm(h, norm_mix_g[l])
        proj = n @ w_in[l]
        q, k, v, g_ret, z, xbc, dt_raw, gate_ret, gate_ssm = jnp.split(proj, split_points, axis=-1)

        q = rotary(q.reshape(b, seq_len, RET_HEADS, RET_QK_HEAD), pos)
        k = rotary(k.reshape(b, seq_len, RET_HEADS, RET_QK_HEAD), pos) * (RET_QK_HEAD ** -0.5)
        v = v.reshape(b, seq_len, RET_HEADS, RET_V_HEAD)
        o = retention_chunked(pad_front(q), pad_front(k), pad_front(v), log_gamma)[:, LEAD:]
        o = _rms(o).reshape(b, seq_len, RET_V)
        y_ret = (jax.nn.silu(g_ret.astype(f32)) * o) @ w_ret_o[l]

        xbc = jax.nn.silu(causal_dwconv(xbc, conv_w[l], conv_b[l]))
        xs, bm, cm = jnp.split(xbc, [SSM_INNER, SSM_INNER + SSM_GROUPS * SSM_STATE], axis=-1)
        xs = xs.reshape(b, seq_len, SSM_HEADS, SSM_HEAD_DIM).astype(f32)
        bm = bm.reshape(b, seq_len, SSM_GROUPS, SSM_STATE)
        cm = cm.reshape(b, seq_len, SSM_GROUPS, SSM_STATE)
        dt = jax.nn.softplus(dt_raw.astype(f32) + dt_bias[l].astype(f32))
        a_cont = -jnp.exp(a_log[l].astype(f32))
        y = ssd_chunked(pad_front(xs * dt[..., None]), pad_front(dt * a_cont),
                        pad_front(bm), pad_front(cm))[:, LEAD:]
        y = y + d_skip[l].astype(f32)[:, None] * xs
        y = y.reshape(b, seq_len, SSM_INNER) * jax.nn.silu(z.astype(f32))
        y = _rms(y.reshape(b, seq_len, SSM_GROUPS, SSM_INNER // SSM_GROUPS)).reshape(
            b, seq_len, SSM_INNER) * ssm_norm_g[l].astype(f32)
        y_ssm = y @ w_ssm_o[l]

        merged = jax.nn.sigmoid(gate_ret.astype(f32)) * y_ret + jax.nn.sigmoid(gate_ssm.astype(f32)) * y_ssm
        h = h + (merged @ w_out[l]).astype(h.dtype)

        h = h + peer_ffn(rmsnorm(h, norm_ffn_g[l]), peer_w_q[l], peer_sub_keys[l],
                         peer_u[l], peer_v[l]).astype(h.dtype)

    return rmsnorm(h, norm_final_g)[:, N_META:]
```

```python
import functools
import math

import jax
import jax.numpy as jnp
import numpy as np
from jax import lax
from jax.experimental import pallas as pl
from jax.experimental.pallas import tpu as pltpu

N_META = 16
CHUNK = 128
LEAD = (-N_META) % CHUNK
NORM_EPS = 1e-6
ROPE_BASE = 10000.0

RET_HEADS = 8
SSM_HEAD_DIM = 64
SSM_GROUPS = 8
SSM_STATE = 128
CONV_WIDTH = 4
CONV_TAIL = 8

PEER_HEADS = 8
PEER_NKEYS = 128
PEER_TOPK = 16
PEER_HALF = 128

LANES = 128
VMEM_LIMIT = 56 * 1024 * 1024

F32 = jnp.float32
BF16 = jnp.bfloat16
NEG_INF = float("-inf")


def _pick(n, prefs):
    for p in prefs:
        if n % p == 0:
            return p
    raise ValueError(f"no tile for {n} in {prefs}")


def _params(sem):
    return pltpu.CompilerParams(dimension_semantics=sem, vmem_limit_bytes=VMEM_LIMIT)


def _silu(v):
    return v * jax.nn.sigmoid(v)


def _rmsnorm_kernel(h_ref, g_ref, o_ref):
    h = h_ref[...]
    ms = jnp.mean(h * h, axis=-1, keepdims=True)
    o_ref[...] = (h * lax.rsqrt(ms + NORM_EPS) * g_ref[...]).astype(o_ref.dtype)


def _rmsnorm(h, g):
    m, d = h.shape
    tm = _pick(m, (512, 256, 128))
    return pl.pallas_call(
        _rmsnorm_kernel,
        grid=(m // tm,),
        in_specs=[pl.BlockSpec((tm, d), lambda i: (i, 0)),
                  pl.BlockSpec((1, d), lambda i: (0, 0))],
        out_specs=pl.BlockSpec((tm, d), lambda i: (i, 0)),
        out_shape=jax.ShapeDtypeStruct((m, d), BF16),
        compiler_params=_params(("parallel",)),
        name="rmsnorm",
    )(h, g.reshape(1, d).astype(F32))


def _mm_kernel(a_ref, w_ref, o_ref):
    o_ref[...] = jnp.dot(a_ref[...], w_ref[...], preferred_element_type=F32).astype(o_ref.dtype)


def _matmul(a, w, out_dtype, name):
    m, k = a.shape
    n = w.shape[1]
    tm = _pick(m, (1280, 640, 512, 384, 256, 128))
    tn = _pick(n, (1024, 512, 256, 128))
    return pl.pallas_call(
        _mm_kernel,
        grid=(m // tm, n // tn),
        in_specs=[pl.BlockSpec((tm, k), lambda i, j: (i, 0)),
                  pl.BlockSpec((k, tn), lambda i, j: (0, j))],
        out_specs=pl.BlockSpec((tm, tn), lambda i, j: (i, j)),
        out_shape=jax.ShapeDtypeStruct((m, n), out_dtype),
        compiler_params=_params(("parallel", "parallel")),
        name=name,
    )(a, w)


def _ret_kernel(lg_ref, q_ref, k_ref, v_ref, g_ref, cos_ref, sin_ref, o_ref, state_ref, *, dk, dv):
    c = pl.program_id(1)

    @pl.when(c == 0)
    def _():
        state_ref[...] = jnp.zeros_like(state_ref)

    cos = cos_ref[...]
    sin = sin_ref[...]
    row = lax.broadcasted_iota(jnp.int32, (CHUNK, CHUNK), 0).astype(F32)
    col = lax.broadcasted_iota(jnp.int32, (CHUNK, CHUNK), 1).astype(F32)
    rel = row - col
    half = dk // 2

    def rot(t):
        t1 = t[:, :half]
        t2 = t[:, half:]
        return jnp.concatenate([t1 * cos - t2 * sin, t1 * sin + t2 * cos], axis=1)

    for h in range(RET_HEADS):
        lg = lg_ref[h]
        dmask = jnp.where(rel >= 0, jnp.exp(lg * jnp.maximum(rel, 0.0)), 0.0)
        xi = jnp.exp(lg * (row + 1.0))
        zeta = jnp.exp(lg * (CHUNK - 1.0 - row))
        decay = jnp.exp(jnp.full((1, dv), lg * CHUNK, F32))

        qr = rot(q_ref[0, :, h * dk:(h + 1) * dk].astype(F32))
        kr = rot(k_ref[0, :, h * dk:(h + 1) * dk].astype(F32)) * (dk ** -0.5)
        v = v_ref[0, :, h * dv:(h + 1) * dv]
        qb = qr.astype(BF16)
        s = lax.dot_general(qb, kr.astype(BF16), (((1,), (1,)), ((), ())),
                            preferred_element_type=F32) * dmask
        inner = jnp.dot(s.astype(BF16), v, preferred_element_type=F32)
        st = state_ref[h]
        cross = jnp.dot(qb, st.astype(BF16), preferred_element_type=F32) * jnp.tile(xi, (1, dv // CHUNK))
        kz = (kr * jnp.tile(zeta, (1, dk // CHUNK))).astype(BF16)
        state_ref[h] = st * decay + lax.dot_general(kz, v, (((0,), (0,)), ((), ())),
                                                    preferred_element_type=F32)
        o = inner + cross
        ms = jnp.mean(o * o, axis=-1, keepdims=True)
        on = o * lax.rsqrt(ms + NORM_EPS)
        g = g_ref[0, :, h * dv:(h + 1) * dv].astype(F32)
        o_ref[0, :, h * dv:(h + 1) * dv] = (_silu(g) * on).astype(o_ref.dtype)


def _retention(p_qk, p_vg, cos, sin, log_gamma, b, lp):
    d_qk = p_qk.shape[-1] // 2
    d_v = p_vg.shape[-1] // 2
    dk = d_qk // RET_HEADS
    dv = d_v // RET_HEADS
    nc = lp // CHUNK
    p_qk = p_qk.reshape(b, lp, 2 * d_qk)
    p_vg = p_vg.reshape(b, lp, 2 * d_v)
    out = pl.pallas_call(
        functools.partial(_ret_kernel, dk=dk, dv=dv),
        grid=(b, nc),
        in_specs=[pl.BlockSpec(memory_space=pltpu.SMEM),
                  pl.BlockSpec((1, CHUNK, d_qk), lambda i, c: (i, c, 0)),
                  pl.BlockSpec((1, CHUNK, d_qk), lambda i, c: (i, c, 1)),
                  pl.BlockSpec((1, CHUNK, d_v), lambda i, c: (i, c, 0)),
                  pl.BlockSpec((1, CHUNK, d_v), lambda i, c: (i, c, 1)),
                  pl.BlockSpec((CHUNK, dk // 2), lambda i, c: (c, 0)),
                  pl.BlockSpec((CHUNK, dk // 2), lambda i, c: (c, 0))],
        out_specs=pl.BlockSpec((1, CHUNK, d_v), lambda i, c: (i, c, 0)),
        out_shape=jax.ShapeDtypeStruct((b, lp, d_v), BF16),
        scratch_shapes=[pltpu.VMEM((RET_HEADS, dk, dv), F32)],
        compiler_params=_params(("parallel", "arbitrary")),
        name="retention",
    )(log_gamma, p_qk, p_qk, p_vg, p_vg, cos, sin)
    return out.reshape(b * lp, d_v)


def _ssd_kernel(z_ref, x_ref, bm_ref, cm_ref, dt_ref, cw_ref, cb_ref, dtb_ref, alog_ref, dskip_ref,
                ng_ref, exp_ref, o_ref, xext_ref, xact_ref, state_ref, *, d_in, gn):
    c = pl.program_id(1)
    gw = d_in // SSM_GROUPS
    hg = gw // SSM_HEAD_DIM
    d_conv = d_in + 2 * gn

    @pl.when(c == 0)
    def _():
        state_ref[...] = jnp.zeros_like(state_ref)
        xext_ref[0:CONV_TAIL, :] = jnp.zeros((CONV_TAIL, d_conv), F32)

    xext_ref[CONV_TAIL:CONV_TAIL + CHUNK, 0:d_in] = x_ref[0].astype(F32)
    xext_ref[CONV_TAIL:CONV_TAIL + CHUNK, d_in:d_in + gn] = bm_ref[0].astype(F32)
    xext_ref[CONV_TAIL:CONV_TAIL + CHUNK, d_in + gn:d_conv] = cm_ref[0].astype(F32)

    row_i = lax.broadcasted_iota(jnp.int32, (CHUNK, LANES), 0)
    col_i = lax.broadcasted_iota(jnp.int32, (CHUNK, LANES), 1)
    valid = (c * CHUNK + row_i) >= LEAD
    tri = row_i >= col_i

    cblk = 512
    for j in range(d_conv // cblk):
        cs = slice(j * cblk, (j + 1) * cblk)
        acc = jnp.broadcast_to(cb_ref[:, cs], (CHUNK, cblk))
        for w in range(CONV_WIDTH):
            off = CONV_TAIL - (CONV_WIDTH - 1) + w
            acc = acc + cw_ref[w:w + 1, cs] * xext_ref[off:off + CHUNK, cs]
        act = _silu(acc)
        xact_ref[:, cs] = jnp.where(jnp.tile(valid, (1, cblk // LANES)), act, 0.0)
    xext_ref[0:CONV_TAIL, :] = xext_ref[CHUNK:CHUNK + CONV_TAIL, :]

    dtr = dt_ref[0] + dtb_ref[...]
    dt = jnp.maximum(dtr, 0.0) + jnp.log1p(jnp.exp(-jnp.abs(dtr)))
    dt = jnp.where(valid, dt, 0.0)
    a = dt * (-jnp.exp(alog_ref[...]))
    ltri = jnp.where(tri, 1.0, 0.0)
    acs = jnp.dot(ltri, a, preferred_element_type=F32, precision=lax.Precision.HIGHEST)
    acs_t = acs.T
    acs_last = acs[CHUNK - 1:CHUNK, :]
    eacs = jnp.exp(acs)
    dtdec = dt * jnp.exp(acs_last - acs)
    stack = jnp.concatenate([dt, eacs, dtdec], axis=0).astype(BF16)

    lane_lo = col_i < SSM_HEAD_DIM
    for g in range(SSM_GROUPS):
        gs = slice(g * gw, (g + 1) * gw)
        ex = jnp.dot(stack, exp_ref[:, gs], preferred_element_type=F32)
        dt_x = ex[0:CHUNK]
        eacs_x = ex[CHUNK:2 * CHUNK]
        dtdec_x = ex[2 * CHUNK:3 * CHUNK]
        xs = xact_ref[:, gs]
        bmat = xact_ref[:, d_in + g * SSM_STATE:d_in + (g + 1) * SSM_STATE].astype(BF16)
        cmat = xact_ref[:, d_in + gn + g * SSM_STATE:d_in + gn + (g + 1) * SSM_STATE].astype(BF16)
        xdt = xs * dt_x
        cb = lax.dot_general(cmat, bmat, (((1,), (1,)), ((), ())), preferred_element_type=F32)
        tiles = []
        for t in range(gw // LANES):
            xt = xdt[:, t * LANES:(t + 1) * LANES]
            acc = None
            for s in range(LANES // SSM_HEAD_DIM):
                hh = g * hg + t * (LANES // SSM_HEAD_DIM) + s
                diff = acs[:, hh:hh + 1] - acs_t[hh:hh + 1, :]
                gm = (cb * jnp.where(tri, jnp.exp(diff), 0.0)).astype(BF16)
                xh = jnp.where(lane_lo if s == 0 else jnp.logical_not(lane_lo), xt, 0.0).astype(BF16)
                part = jnp.dot(gm, xh, preferred_element_type=F32)
                acc = part if acc is None else acc + part
            tiles.append(acc)
        y_diag = jnp.concatenate(tiles, axis=1)
        st = state_ref[g]
        y_off = jnp.dot(cmat, st.astype(BF16), preferred_element_type=F32) * eacs_x
        xw = (xs * dtdec_x).astype(BF16)
        state_ref[g] = st * eacs_x[CHUNK - 1:CHUNK, :] + lax.dot_general(
            bmat, xw, (((0,), (0,)), ((), ())), preferred_element_type=F32)
        y = y_diag + y_off + dskip_ref[:, gs] * xs
        y = y * _silu(z_ref[0, :, gs].astype(F32))
        ms = jnp.mean(y * y, axis=-1, keepdims=True)
        o_ref[0, :, gs] = (y * lax.rsqrt(ms + NORM_EPS) * ng_ref[:, gs]).astype(o_ref.dtype)


def _ssd(p_ssm, dt_raw, conv_w, conv_b, dt_bias, a_log, d_skip, norm_g, b, lp):
    d_in = norm_g.shape[-1]
    gn = SSM_GROUPS * SSM_STATE
    n_heads = d_in // SSM_HEAD_DIM
    nc = lp // CHUNK
    d_conv = d_in + 2 * gn
    p_ssm = p_ssm.reshape(b, lp, 2 * d_in + 2 * gn)
    dt_raw = dt_raw.reshape(b, lp, LANES)
    pad = LANES - n_heads
    dtb = jnp.pad(dt_bias.astype(F32), (0, pad)).reshape(1, LANES)
    alog = jnp.pad(a_log.astype(F32), (0, pad)).reshape(1, LANES)
    dskip = jnp.repeat(d_skip.astype(F32), SSM_HEAD_DIM).reshape(1, d_in)
    expand = (jnp.arange(LANES)[:, None] == (jnp.arange(d_in)[None, :] // SSM_HEAD_DIM)).astype(BF16)
    full = lambda shape: pl.BlockSpec(shape, lambda i, c: (0,) * len(shape))
    kern = functools.partial(_ssd_kernel, d_in=d_in, gn=gn)
    out = pl.pallas_call(
        kern,
        grid=(b, nc),
        in_specs=[pl.BlockSpec((1, CHUNK, d_in), lambda i, c: (i, c, 0)),
                  pl.BlockSpec((1, CHUNK, d_in), lambda i, c: (i, c, 1)),
                  pl.BlockSpec((1, CHUNK, gn), lambda i, c: (i, c, 2 * d_in // gn)),
                  pl.BlockSpec((1, CHUNK, gn), lambda i, c: (i, c, 2 * d_in // gn + 1)),
                  pl.BlockSpec((1, CHUNK, LANES), lambda i, c: (i, c, 0)),
                  full((CONV_WIDTH, d_conv)), full((1, d_conv)), full((1, LANES)), full((1, LANES)),
                  full((1, d_in)), full((1, d_in)), full((LANES, d_in))],
        out_specs=pl.BlockSpec((1, CHUNK, d_in), lambda i, c: (i, c, 0)),
        out_shape=jax.ShapeDtypeStruct((b, lp, d_in), BF16),
        scratch_shapes=[pltpu.VMEM((CHUNK + CONV_TAIL, d_conv), F32),
                        pltpu.VMEM((CHUNK, d_conv), F32),
                        pltpu.VMEM((SSM_GROUPS, SSM_STATE, d_in // SSM_GROUPS), F32)],
        compiler_params=_params(("parallel", "arbitrary")),
        name="ssd",
    )(p_ssm, p_ssm, p_ssm, p_ssm, dt_raw, conv_w.astype(F32), conv_b.reshape(1, d_conv).astype(F32),
      dtb, alog, dskip, norm_g.reshape(1, d_in).astype(F32), expand)
    return out.reshape(b * lp, d_in)


def _merge_kernel(yr_ref, ys_ref, wr_ref, ws_ref, gr_ref, gs_ref, o_ref):
    yr = jnp.dot(yr_ref[...], wr_ref[...], preferred_element_type=F32)
    ys = jnp.dot(ys_ref[...], ws_ref[...], preferred_element_type=F32)
    o_ref[...] = (jax.nn.sigmoid(gr_ref[...].astype(F32)) * yr
                  + jax.nn.sigmoid(gs_ref[...].astype(F32)) * ys).astype(o_ref.dtype)


def _merge(y_ret, y_ssm, w_ret_o, w_ssm_o, p_gate):
    m, kr = y_ret.shape
    ks = y_ssm.shape[1]
    d = w_ret_o.shape[1]
    tm = _pick(m, (640, 512, 384, 256, 128))
    tn = _pick(d, (512, 256, 128))
    nj = d // tn
    return pl.pallas_call(
        _merge_kernel,
        grid=(m // tm, nj),
        in_specs=[pl.BlockSpec((tm, kr), lambda i, j: (i, 0)),
                  pl.BlockSpec((tm, ks), lambda i, j: (i, 0)),
                  pl.BlockSpec((kr, tn), lambda i, j: (0, j)),
                  pl.BlockSpec((ks, tn), lambda i, j: (0, j)),
                  pl.BlockSpec((tm, tn), lambda i, j: (i, j)),
                  pl.BlockSpec((tm, tn), lambda i, j: (i, nj + j))],
        out_specs=pl.BlockSpec((tm, tn), lambda i, j: (i, j)),
        out_shape=jax.ShapeDtypeStruct((m, d), BF16),
        compiler_params=_params(("parallel", "parallel")),
        name="merge",
    )(y_ret, y_ssm, w_ret_o, w_ssm_o, p_gate, p_gate)


def _out_kernel(m_ref, w_ref, h_ref, g_ref, h2_ref, xt_ref):
    h2 = h_ref[...] + jnp.dot(m_ref[...], w_ref[...], preferred_element_type=F32)
    h2_ref[...] = h2
    ms = jnp.mean(h2 * h2, axis=-1, keepdims=True)
    xn = h2 * lax.rsqrt(ms + NORM_EPS) * g_ref[...]
    xt_ref[...] = xn.T.astype(xt_ref.dtype)


def _out_proj(merged, w_out, h, g):
    m, d = h.shape
    tm = _pick(m, (256, 128))
    return pl.pallas_call(
        _out_kernel,
        grid=(m // tm,),
        in_specs=[pl.BlockSpec((tm, d), lambda i: (i, 0)),
                  pl.BlockSpec((d, d), lambda i: (0, 0)),
                  pl.BlockSpec((tm, d), lambda i: (i, 0)),
                  pl.BlockSpec((1, d), lambda i: (0, 0))],
        out_specs=[pl.BlockSpec((tm, d), lambda i: (i, 0)),
                   pl.BlockSpec((d, tm), lambda i: (0, i))],
        out_shape=[jax.ShapeDtypeStruct((m, d), F32),
                   jax.ShapeDtypeStruct((d, m), BF16)],
        compiler_params=_params(("parallel",)),
        name="out_proj",
    )(merged, w_out, h, g.reshape(1, d).astype(F32))


def _top_values(x, k):
    vals = []
    cur = x
    for r in range(k):
        m = jnp.max(cur, axis=0, keepdims=True)
        vals.append(m)
        if r + 1 < k:
            cur = jnp.where(cur == m, NEG_INF, cur)
    return vals


def _route_kernel(xt_ref, wqt_ref, keys_ref, sa_ref, thr_ref):
    qt = jnp.dot(wqt_ref[...], xt_ref[...], preferred_element_type=F32).astype(BF16)
    thr_rows = []
    for h in range(PEER_HEADS):
        s = []
        top = []
        for c in range(2):
            r0 = (h * 2 + c) * PEER_HALF
            sc = jnp.dot(keys_ref[h, c], qt[r0:r0 + PEER_HALF, :], preferred_element_type=F32)
            s.append(sc)
            top.append(_top_values(sc, PEER_TOPK))
        a, bv = top
        b16 = jnp.concatenate(bv, axis=0)
        b8 = b16[0:8]
        cand = jnp.concatenate([a[0] + b16] + [a[k] + b8 for k in range(1, PEER_TOPK)], axis=0)
        best = _top_values(cand, PEER_TOPK)
        zsum = jnp.ones_like(best[0])
        for r in range(1, PEER_TOPK):
            zsum = zsum + jnp.exp(best[r] - best[0])
        thr_rows.append(best[PEER_TOPK - 1])
        sa_ref[h, 0] = s[0]
        sa_ref[h, 1] = s[1]
        sa_ref[h, 2] = jnp.exp(s[0] - a[0]) / zsum
        sa_ref[h, 3] = jnp.exp(s[1] - bv[0])
    thr_ref[...] = jnp.concatenate(thr_rows, axis=0)


def _route(xt, wq_t, keys):
    d, m = xt.shape
    tb = _pick(m, (256, 128))
    return pl.pallas_call(
        _route_kernel,
        grid=(m // tb,),
        in_specs=[pl.BlockSpec((d, tb), lambda i: (0, i)),
                  pl.BlockSpec(wq_t.shape, lambda i: (0, 0)),
                  pl.BlockSpec(keys.shape, lambda i: (0, 0, 0, 0))],
        out_specs=[pl.BlockSpec((PEER_HEADS, 4, PEER_NKEYS, tb), lambda i: (0, 0, 0, i)),
                   pl.BlockSpec((PEER_HEADS, tb), lambda i: (0, i))],
        out_shape=[jax.ShapeDtypeStruct((PEER_HEADS, 4, PEER_NKEYS, m), F32),
                   jax.ShapeDtypeStruct((PEER_HEADS, m), F32)],
        compiler_params=_params(("parallel",)),
        name="peer_route",
    )(xt, wq_t, keys)


def _peer_kernel(xt_ref, u_ref, vt_ref, sa_ref, thr_ref, o_ref, st_ref, a_ref, *, ec, tb):
    e = pl.program_id(1)
    st_ref[...] = jnp.dot(u_ref[...], xt_ref[...], preferred_element_type=F32)
    n_i = ec // PEER_NKEYS
    i0 = pl.multiple_of(e * n_i, n_i)
    for lb in range(tb // LANES):
        ls = slice(lb * LANES, (lb + 1) * LANES)
        s1g = [sa_ref[h, 0, pl.ds(i0, n_i), ls] for h in range(PEER_HEADS)]
        e1g = [sa_ref[h, 2, pl.ds(i0, n_i), ls] for h in range(PEER_HEADS)]
        for ii in range(n_i):
            rs = slice(ii * PEER_NKEYS, (ii + 1) * PEER_NKEYS)
            w = jnp.zeros((PEER_NKEYS, LANES), F32)
            for h in range(PEER_HEADS):
                tile = sa_ref[h, 1, :, ls] + s1g[h][ii:ii + 1, :]
                w = w + jnp.where(tile >= thr_ref[h:h + 1, ls], sa_ref[h, 3, :, ls] * e1g[h][ii:ii + 1, :], 0.0)
            sc = st_ref[rs, ls]
            act = 0.5 * sc * (1.0 + lax.erf(sc * (2.0 ** -0.5)))
            a_ref[rs, ls] = (w * act).astype(BF16)
    part = jnp.dot(vt_ref[...], a_ref[...], preferred_element_type=F32)

    @pl.when(e == 0)
    def _():
        o_ref[...] = part

    @pl.when(e != 0)
    def _():
        o_ref[...] += part


def _peer(xt, u, vt, sa, thr):
    d, m = xt.shape
    ne = u.shape[0]
    tb = _pick(m, (512, 256, 128))
    ec = 1024
    return pl.pallas_call(
        functools.partial(_peer_kernel, ec=ec, tb=tb),
        grid=(m // tb, ne // ec),
        in_specs=[pl.BlockSpec((d, tb), lambda i, e: (0, i)),
                  pl.BlockSpec((ec, d), lambda i, e: (e, 0)),
                  pl.BlockSpec((d, ec), lambda i, e: (0, e)),
                  pl.BlockSpec((PEER_HEADS, 4, PEER_NKEYS, tb), lambda i, e: (0, 0, 0, i)),
                  pl.BlockSpec((PEER_HEADS, tb), lambda i, e: (0, i))],
        out_specs=pl.BlockSpec((d, tb), lambda i, e: (0, i)),
        out_shape=jax.ShapeDtypeStruct((d, m), F32),
        scratch_shapes=[pltpu.VMEM((ec, tb), F32), pltpu.VMEM((ec, tb), BF16)],
        compiler_params=_params(("parallel", "arbitrary")),
        name="peer_experts",
    )(xt, u, vt, sa, thr)


def _final_kernel(h_ref, pt_ref, g_ref, o_ref):
    h = h_ref[0] + pt_ref[...].T
    ms = jnp.mean(h * h, axis=-1, keepdims=True)
    o_ref[0] = h * lax.rsqrt(ms + NORM_EPS) * g_ref[...]


def _final(h2, peer_t, g, b, lp, seq):
    d = h2.shape[-1]
    nc = lp // CHUNK
    skip = (LEAD + N_META) // CHUNK
    return pl.pallas_call(
        _final_kernel,
        grid=(b, seq // CHUNK),
        in_specs=[pl.BlockSpec((1, CHUNK, d), lambda i, c: (i, c + skip, 0)),
                  pl.BlockSpec((d, CHUNK), lambda i, c: (0, i * nc + c + skip)),
                  pl.BlockSpec((1, d), lambda i, c: (0, 0))],
        out_specs=pl.BlockSpec((1, CHUNK, d), lambda i, c: (i, c, 0)),
        out_shape=jax.ShapeDtypeStruct((b, seq, d), F32),
        compiler_params=_params(("parallel", "parallel")),
        name="final_norm",
    )(h2.reshape(b, lp, d), peer_t, g.reshape(1, d).astype(F32))


def kernel(x, meta_tokens, norm_mix_g, w_in, conv_w, conv_b, dt_bias, a_log, d_skip, ssm_norm_g,
           w_ret_o, w_ssm_o, w_out, norm_ffn_g, peer_w_q, peer_sub_keys, peer_u, peer_v, norm_final_g):
    b, seq, d = x.shape
    assert seq % CHUNK == 0 and (LEAD + N_META) % CHUNK == 0
    lp = LEAD + N_META + seq
    m = b * lp
    d_qk = d
    d_v = w_ret_o.shape[1]
    d_in = ssm_norm_g.shape[-1]
    gn = SSM_GROUPS * SSM_STATE
    n_heads = d_in // SSM_HEAD_DIM
    dk = d_qk // RET_HEADS

    h0 = jnp.concatenate([jnp.zeros((b, LEAD, d), x.dtype),
                          jnp.broadcast_to(meta_tokens.astype(x.dtype)[None], (b, N_META, d)), x], axis=1)
    h0 = h0.reshape(m, d)

    w = w_in[0]
    o_q, o_k, o_v, o_g, o_z, o_xbc = 0, d_qk, 2 * d_qk, 2 * d_qk + d_v, 2 * d_qk + 2 * d_v, 2 * d_qk + 2 * d_v + d_in
    o_dt = o_xbc + d_in + 2 * gn
    o_gate = o_dt + n_heads
    perm = np.concatenate([hh * dk + np.concatenate([np.arange(0, dk, 2), np.arange(1, dk, 2)])
                           for hh in range(RET_HEADS)])
    w_qk = jnp.concatenate([w[:, o_q:o_k][:, perm], w[:, o_k:o_v][:, perm]], axis=1).astype(BF16)
    w_vg = w[:, o_v:o_z].astype(BF16)
    w_ssm = w[:, o_z:o_dt].astype(BF16)
    w_dt = jnp.pad(w[:, o_dt:o_gate], ((0, 0), (0, LANES - n_heads))).astype(BF16)
    w_gate = w[:, o_gate:].astype(BF16)

    pos = jnp.maximum(jnp.arange(lp, dtype=F32) - LEAD, 0.0)
    inv = ROPE_BASE ** (-jnp.arange(dk // 2, dtype=F32) / (dk // 2))
    ang = pos[:, None] * inv[None, :]
    cos, sin = jnp.cos(ang), jnp.sin(ang)
    log_gamma = jnp.log(1.0 - 2.0 ** (-5.0 - jnp.arange(RET_HEADS, dtype=F32)))

    n = _rmsnorm(h0, norm_mix_g[0])
    p_qk = _matmul(n, w_qk, BF16, "proj_qk")
    p_vg = _matmul(n, w_vg, BF16, "proj_vg")
    p_ssm = _matmul(n, w_ssm, BF16, "proj_ssm")
    p_gate = _matmul(n, w_gate, BF16, "proj_gate")
    dt_raw = _matmul(n, w_dt, F32, "proj_dt")

    y_ret = _retention(p_qk, p_vg, cos, sin, log_gamma, b, lp)
    y_ssm = _ssd(p_ssm, dt_raw, conv_w[0], conv_b[0], dt_bias[0], a_log[0], d_skip[0], ssm_norm_g[0], b, lp)
    merged = _merge(y_ret, y_ssm, w_ret_o[0].astype(BF16), w_ssm_o[0].astype(BF16), p_gate)
    h2, xt = _out_proj(merged, w_out[0].astype(BF16), h0, norm_ffn_g[0])

    sa, thr = _route(xt, peer_w_q[0].T.astype(BF16), peer_sub_keys[0].astype(BF16))
    peer_t = _peer(xt, peer_u[0].astype(BF16), peer_v[0].T.astype(BF16), sa, thr)
    return _final(h2, peer_t, norm_final_g, b, lp, seq)
```

```python
import functools
import math

import jax
import jax.numpy as jnp
import numpy as np
from jax import lax
from jax.experimental import pallas as pl
from jax.experimental.pallas import tpu as pltpu

N_META = 16
CHUNK = 128
LEAD = (-N_META) % CHUNK
NORM_EPS = 1e-6
ROPE_BASE = 10000.0

RET_HEADS = 8
SSM_HEAD_DIM = 64
SSM_GROUPS = 8
SSM_STATE = 128
CONV_WIDTH = 4
CONV_TAIL = 16

PEER_HEADS = 8
PEER_NKEYS = 128
PEER_TOPK = 16
PEER_HALF = 128

LANES = 128
VMEM_LIMIT = 56 * 1024 * 1024

F32 = jnp.float32
BF16 = jnp.bfloat16
F8 = jnp.float8_e4m3fn
NEG_INF = float("-inf")


def _pick(n, prefs):
    for p in prefs:
        if n % p == 0:
            return p
    raise ValueError(f"no tile for {n} in {prefs}")


def _params(sem):
    return pltpu.CompilerParams(dimension_semantics=sem, vmem_limit_bytes=VMEM_LIMIT)


def _silu(v):
    return v * jax.nn.sigmoid(v)


def _rmsnorm_kernel(h_ref, g_ref, o_ref):
    h = h_ref[...]
    ms = jnp.mean(h * h, axis=-1, keepdims=True)
    o_ref[...] = (h * lax.rsqrt(ms + NORM_EPS) * g_ref[...]).astype(o_ref.dtype)


def _rmsnorm(h, g):
    m, d = h.shape
    tm = _pick(m, (512, 256, 128))
    return pl.pallas_call(
        _rmsnorm_kernel,
        grid=(m // tm,),
        in_specs=[pl.BlockSpec((tm, d), lambda i: (i, 0)),
                  pl.BlockSpec((1, d), lambda i: (0, 0))],
        out_specs=pl.BlockSpec((tm, d), lambda i: (i, 0)),
        out_shape=jax.ShapeDtypeStruct((m, d), BF16),
        compiler_params=_params(("parallel",)),
        name="rmsnorm",
    )(h, g.reshape(1, d).astype(F32))


def _mm_kernel(a_ref, w_ref, o_ref):
    o_ref[...] = jnp.dot(a_ref[...], w_ref[...], preferred_element_type=F32).astype(o_ref.dtype)


def _matmul(a, w, out_dtype, name):
    m, k = a.shape
    n = w.shape[1]
    tm = _pick(m, (1280, 640, 512, 384, 256, 128))
    tn = _pick(n, (1024, 512, 256, 128))
    return pl.pallas_call(
        _mm_kernel,
        grid=(m // tm, n // tn),
        in_specs=[pl.BlockSpec((tm, k), lambda i, j: (i, 0)),
                  pl.BlockSpec((k, tn), lambda i, j: (0, j))],
        out_specs=pl.BlockSpec((tm, tn), lambda i, j: (i, j)),
        out_shape=jax.ShapeDtypeStruct((m, n), out_dtype),
        compiler_params=_params(("parallel", "parallel")),
        name=name,
    )(a, w)


def _ret_kernel(lg_ref, q_ref, k_ref, v_ref, g_ref, cos_ref, sin_ref, o_ref, state_ref, *, dk, dv):
    c = pl.program_id(1)

    @pl.when(c == 0)
    def _():
        state_ref[...] = jnp.zeros_like(state_ref)

    cos = cos_ref[...]
    sin = sin_ref[...]
    row = lax.broadcasted_iota(jnp.int32, (CHUNK, CHUNK), 0).astype(F32)
    col = lax.broadcasted_iota(jnp.int32, (CHUNK, CHUNK), 1).astype(F32)
    rel = row - col
    half = dk // 2

    def rot(t):
        t1 = t[:, :half]
        t2 = t[:, half:]
        return jnp.concatenate([t1 * cos - t2 * sin, t1 * sin + t2 * cos], axis=1)

    for h in range(RET_HEADS):
        lg = lg_ref[h]
        dmask = jnp.where(rel >= 0, jnp.exp(lg * jnp.maximum(rel, 0.0)), 0.0)
        xi = jnp.exp(lg * (row + 1.0))
        zeta = jnp.exp(lg * (CHUNK - 1.0 - row))
        decay = jnp.exp(jnp.full((1, dv), lg * CHUNK, F32))

        qr = rot(q_ref[0, :, h * dk:(h + 1) * dk].astype(F32))
        kr = rot(k_ref[0, :, h * dk:(h + 1) * dk].astype(F32)) * (dk ** -0.5)
        v = v_ref[0, :, h * dv:(h + 1) * dv]
        qb = qr.astype(BF16)
        s = lax.dot_general(qb, kr.astype(BF16), (((1,), (1,)), ((), ())),
                            preferred_element_type=F32) * dmask
        inner = jnp.dot(s.astype(BF16), v, preferred_element_type=F32)
        st = state_ref[h]
        cross = jnp.dot(qb, st.astype(BF16), preferred_element_type=F32) * jnp.tile(xi, (1, dv // CHUNK))
        kz = (kr * jnp.tile(zeta, (1, dk // CHUNK))).astype(BF16)
        state_ref[h] = st * decay + lax.dot_general(kz, v, (((0,), (0,)), ((), ())),
                                                    preferred_element_type=F32)
        o = inner + cross
        ms = jnp.mean(o * o, axis=-1, keepdims=True)
        on = o * lax.rsqrt(ms + NORM_EPS)
        g = g_ref[0, :, h * dv:(h + 1) * dv].astype(F32)
        o_ref[0, :, h * dv:(h + 1) * dv] = (_silu(g) * on).astype(o_ref.dtype)


def _retention(p_qk, p_vg, cos, sin, log_gamma, b, lp):
    d_qk = p_qk.shape[-1] // 2
    d_v = p_vg.shape[-1] // 2
    dk = d_qk // RET_HEADS
    dv = d_v // RET_HEADS
    nc = lp // CHUNK
    p_qk = p_qk.reshape(b, lp, 2 * d_qk)
    p_vg = p_vg.reshape(b, lp, 2 * d_v)
    out = pl.pallas_call(
        functools.partial(_ret_kernel, dk=dk, dv=dv),
        grid=(b, nc),
        in_specs=[pl.BlockSpec(memory_space=pltpu.SMEM),
                  pl.BlockSpec((1, CHUNK, d_qk), lambda i, c: (i, c, 0)),
                  pl.BlockSpec((1, CHUNK, d_qk), lambda i, c: (i, c, 1)),
                  pl.BlockSpec((1, CHUNK, d_v), lambda i, c: (i, c, 0)),
                  pl.BlockSpec((1, CHUNK, d_v), lambda i, c: (i, c, 1)),
                  pl.BlockSpec((CHUNK, dk // 2), lambda i, c: (c, 0)),
                  pl.BlockSpec((CHUNK, dk // 2), lambda i, c: (c, 0))],
        out_specs=pl.BlockSpec((1, CHUNK, d_v), lambda i, c: (i, c, 0)),
        out_shape=jax.ShapeDtypeStruct((b, lp, d_v), BF16),
        scratch_shapes=[pltpu.VMEM((RET_HEADS, dk, dv), F32)],
        compiler_params=_params(("parallel", "arbitrary")),
        name="retention",
    )(log_gamma, p_qk, p_qk, p_vg, p_vg, cos, sin)
    return out.reshape(b * lp, d_v)


PROJ_CONV_COLS = 256


def _proj_conv_kernel(a_ref, ap_ref, w_ref, cw_ref, cb_ref, o_ref, *, tiles_per_row):
    i = pl.program_id(0)
    tm, tn = o_ref.shape
    t_in_row = lax.rem(i, tiles_per_row)
    tc = PROJ_CONV_COLS
    row = t_in_row * tm + lax.broadcasted_iota(jnp.int32, (tm, tc), 0)

    def project(c):
        cs = slice(c * tc, (c + 1) * tc)
        x = jnp.dot(a_ref[...], w_ref[:, cs], preferred_element_type=F32)
        xp = jnp.dot(ap_ref[...], w_ref[:, cs], preferred_element_type=F32)
        return jnp.concatenate([jnp.where(t_in_row == 0, 0.0, xp), x], axis=0)

    def finish(c, ext):
        cs = slice(c * tc, (c + 1) * tc)
        acc = jnp.broadcast_to(cb_ref[:, cs], (tm, tc))
        for w in range(CONV_WIDTH):
            back = CONV_WIDTH - 1 - w
            shifted = ext if back == 0 else pltpu.roll(ext, back, axis=0)
            acc = acc + cw_ref[w:w + 1, cs] * shifted[CONV_TAIL:CONV_TAIL + tm, :]
        o_ref[:, cs] = jnp.where(row >= LEAD, _silu(acc), 0.0).astype(o_ref.dtype)

    n_c = tn // tc
    ext = project(0)
    for c in range(n_c):
        nxt = project(c + 1) if c + 1 < n_c else None
        finish(c, ext)
        ext = nxt


def _proj_conv(a, w, conv_w, conv_b, lp):
    m, k = a.shape
    n = w.shape[1]
    tm = _pick(lp, (640, 512, 384, 256, 128))
    tn = _pick(n, (1024, 512, 256, 128))
    sub = tm // CONV_TAIL
    return pl.pallas_call(
        functools.partial(_proj_conv_kernel, tiles_per_row=lp // tm),
        grid=(m // tm, n // tn),
        in_specs=[pl.BlockSpec((tm, k), lambda i, j: (i, 0)),
                  pl.BlockSpec((CONV_TAIL, k), lambda i, j: (jnp.maximum(i * sub - 1, 0), 0)),
                  pl.BlockSpec((k, tn), lambda i, j: (0, j)),
                  pl.BlockSpec((CONV_WIDTH, tn), lambda i, j: (0, j)),
                  pl.BlockSpec((1, tn), lambda i, j: (0, j))],
        out_specs=pl.BlockSpec((tm, tn), lambda i, j: (i, j)),
        out_shape=jax.ShapeDtypeStruct((m, n), BF16),
        compiler_params=_params(("parallel", "parallel")),
        name="proj_conv",
    )(a, a, w, conv_w.astype(F32), conv_b.reshape(1, n).astype(F32))


def _ssd_kernel(z_ref, x_ref, bm_ref, cm_ref, dt_ref, dtb_ref, alog_ref, dskip_ref,
                ng_ref, exp_ref, o_ref, state_ref, *, d_in):
    c = pl.program_id(1)
    gw = d_in // SSM_GROUPS
    hg = gw // SSM_HEAD_DIM

    @pl.when(c == 0)
    def _():
        state_ref[...] = jnp.zeros_like(state_ref)

    row_i = lax.broadcasted_iota(jnp.int32, (CHUNK, LANES), 0)
    col_i = lax.broadcasted_iota(jnp.int32, (CHUNK, LANES), 1)
    valid = (c * CHUNK + row_i) >= LEAD
    tri = row_i >= col_i

    dtr = dt_ref[0] + dtb_ref[...]
    dt = jnp.maximum(dtr, 0.0) + jnp.log1p(jnp.exp(-jnp.abs(dtr)))
    dt = jnp.where(valid, dt, 0.0)
    a = dt * (-jnp.exp(alog_ref[...]))
    ltri = jnp.where(tri, 1.0, 0.0)
    acs = jnp.dot(ltri, a, preferred_element_type=F32, precision=lax.Precision.HIGHEST)
    acs_t = acs.T
    acs_last = acs[CHUNK - 1:CHUNK, :]
    eacs = jnp.exp(acs)
    dtdec = dt * jnp.exp(acs_last - acs)
    stack = jnp.concatenate([dt, eacs, dtdec], axis=0).astype(BF16)

    lane_lo = col_i < SSM_HEAD_DIM
    for g in range(SSM_GROUPS):
        gs = slice(g * gw, (g + 1) * gw)
        ex = jnp.dot(stack, exp_ref[:, gs], preferred_element_type=F32)
        dt_x = ex[0:CHUNK]
        eacs_x = ex[CHUNK:2 * CHUNK]
        dtdec_x = ex[2 * CHUNK:3 * CHUNK]
        xs = x_ref[0, :, gs].astype(F32)
        bmat = bm_ref[0, :, g * SSM_STATE:(g + 1) * SSM_STATE]
        cmat = cm_ref[0, :, g * SSM_STATE:(g + 1) * SSM_STATE]
        xdt = xs * dt_x
        cb = lax.dot_general(cmat, bmat, (((1,), (1,)), ((), ())), preferred_element_type=F32)
        tiles = []
        for t in range(gw // LANES):
            xt = xdt[:, t * LANES:(t + 1) * LANES]
            acc = None
            for s in range(LANES // SSM_HEAD_DIM):
                hh = g * hg + t * (LANES // SSM_HEAD_DIM) + s
                diff = acs[:, hh:hh + 1] - acs_t[hh:hh + 1, :]
                gm = (cb * jnp.where(tri, jnp.exp(diff), 0.0)).astype(BF16)
                xh = jnp.where(lane_lo if s == 0 else jnp.logical_not(lane_lo), xt, 0.0).astype(BF16)
                part = jnp.dot(gm, xh, preferred_element_type=F32)
                acc = part if acc is None else acc + part
            tiles.append(acc)
        y_diag = jnp.concatenate(tiles, axis=1)
        st = state_ref[g]
        y_off = jnp.dot(cmat, st.astype(BF16), preferred_element_type=F32) * eacs_x
        xw = (xs * dtdec_x).astype(BF16)
        state_ref[g] = st * eacs_x[CHUNK - 1:CHUNK, :] + lax.dot_general(
            bmat, xw, (((0,), (0,)), ((), ())), preferred_element_type=F32)
        y = y_diag + y_off + dskip_ref[:, gs] * xs
        y = y * _silu(z_ref[0, :, gs].astype(F32))
        ms = jnp.mean(y * y, axis=-1, keepdims=True)
        o_ref[0, :, gs] = (y * lax.rsqrt(ms + NORM_EPS) * ng_ref[:, gs]).astype(o_ref.dtype)


def _ssd(p_z, p_xbc, dt_raw, dt_bias, a_log, d_skip, norm_g, b, lp):
    d_in = norm_g.shape[-1]
    gn = SSM_GROUPS * SSM_STATE
    n_heads = d_in // SSM_HEAD_DIM
    nc = lp // CHUNK
    p_z = p_z.reshape(b, lp, d_in)
    p_xbc = p_xbc.reshape(b, lp, d_in + 2 * gn)
    dt_raw = dt_raw.reshape(b, lp, LANES)
    pad = LANES - n_heads
    dtb = jnp.pad(dt_bias.astype(F32), (0, pad)).reshape(1, LANES)
    alog = jnp.pad(a_log.astype(F32), (0, pad)).reshape(1, LANES)
    dskip = jnp.repeat(d_skip.astype(F32), SSM_HEAD_DIM).reshape(1, d_in)
    expand = (jnp.arange(LANES)[:, None] == (jnp.arange(d_in)[None, :] // SSM_HEAD_DIM)).astype(BF16)
    full = lambda shape: pl.BlockSpec(shape, lambda i, c: (0,) * len(shape))
    out = pl.pallas_call(
        functools.partial(_ssd_kernel, d_in=d_in),
        grid=(b, nc),
        in_specs=[pl.BlockSpec((1, CHUNK, d_in), lambda i, c: (i, c, 0)),
                  pl.BlockSpec((1, CHUNK, d_in), lambda i, c: (i, c, 0)),
                  pl.BlockSpec((1, CHUNK, gn), lambda i, c: (i, c, d_in // gn)),
                  pl.BlockSpec((1, CHUNK, gn), lambda i, c: (i, c, d_in // gn + 1)),
                  pl.BlockSpec((1, CHUNK, LANES), lambda i, c: (i, c, 0)),
                  full((1, LANES)), full((1, LANES)),
                  full((1, d_in)), full((1, d_in)), full((LANES, d_in))],
        out_specs=pl.BlockSpec((1, CHUNK, d_in), lambda i, c: (i, c, 0)),
        out_shape=jax.ShapeDtypeStruct((b, lp, d_in), BF16),
        scratch_shapes=[pltpu.VMEM((SSM_GROUPS, SSM_STATE, d_in // SSM_GROUPS), F32)],
        compiler_params=_params(("parallel", "arbitrary")),
        name="ssd",
    )(p_z, p_xbc, p_xbc, p_xbc, dt_raw, dtb, alog, dskip, norm_g.reshape(1, d_in).astype(F32), expand)
    return out.reshape(b * lp, d_in)


def _merge_kernel(yr_ref, ys_ref, wr_ref, ws_ref, gr_ref, gs_ref, o_ref):
    yr = jnp.dot(yr_ref[...], wr_ref[...], preferred_element_type=F32)
    ys = jnp.dot(ys_ref[...], ws_ref[...], preferred_element_type=F32)
    o_ref[...] = (jax.nn.sigmoid(gr_ref[...].astype(F32)) * yr
                  + jax.nn.sigmoid(gs_ref[...].astype(F32)) * ys).astype(o_ref.dtype)


def _merge(y_ret, y_ssm, w_ret_o, w_ssm_o, p_gate):
    m, kr = y_ret.shape
    ks = y_ssm.shape[1]
    d = w_ret_o.shape[1]
    tm = _pick(m, (640, 512, 384, 256, 128))
    tn = _pick(d, (512, 256, 128))
    nj = d // tn
    return pl.pallas_call(
        _merge_kernel,
        grid=(m // tm, nj),
        in_specs=[pl.BlockSpec((tm, kr), lambda i, j: (i, 0)),
                  pl.BlockSpec((tm, ks), lambda i, j: (i, 0)),
                  pl.BlockSpec((kr, tn), lambda i, j: (0, j)),
                  pl.BlockSpec((ks, tn), lambda i, j: (0, j)),
                  pl.BlockSpec((tm, tn), lambda i, j: (i, j)),
                  pl.BlockSpec((tm, tn), lambda i, j: (i, nj + j))],
        out_specs=pl.BlockSpec((tm, tn), lambda i, j: (i, j)),
        out_shape=jax.ShapeDtypeStruct((m, d), BF16),
        compiler_params=_params(("parallel", "parallel")),
        name="merge",
    )(y_ret, y_ssm, w_ret_o, w_ssm_o, p_gate, p_gate)


def _out_kernel(sx_ref, m_ref, w_ref, h_ref, g_ref, h2_ref, xt_ref, xt8_ref):
    h2 = h_ref[...] + jnp.dot(m_ref[...], w_ref[...], preferred_element_type=F32)
    h2_ref[...] = h2
    ms = jnp.mean(h2 * h2, axis=-1, keepdims=True)
    xn_t = (h2 * lax.rsqrt(ms + NORM_EPS) * g_ref[...]).T
    xt_ref[...] = xn_t.astype(xt_ref.dtype)
    xt8_ref[...] = (xn_t * sx_ref[0]).astype(xt8_ref.dtype)


def _pow2_scale(bound):
    bound = jnp.maximum(bound.astype(F32), jnp.finfo(F32).tiny)
    return jnp.exp2(jnp.floor(jnp.log2(float(jnp.finfo(F8).max) / bound)))


def _out_proj(merged, w_out, h, g, sx):
    m, d = h.shape
    tm = _pick(m, (256, 128))
    return pl.pallas_call(
        _out_kernel,
        grid=(m // tm,),
        in_specs=[pl.BlockSpec(memory_space=pltpu.SMEM),
                  pl.BlockSpec((tm, d), lambda i: (i, 0)),
                  pl.BlockSpec((d, d), lambda i: (0, 0)),
                  pl.BlockSpec((tm, d), lambda i: (i, 0)),
                  pl.BlockSpec((1, d), lambda i: (0, 0))],
        out_specs=[pl.BlockSpec((tm, d), lambda i: (i, 0)),
                   pl.BlockSpec((d, tm), lambda i: (0, i)),
                   pl.BlockSpec((d, tm), lambda i: (0, i))],
        out_shape=[jax.ShapeDtypeStruct((m, d), F32),
                   jax.ShapeDtypeStruct((d, m), BF16),
                   jax.ShapeDtypeStruct((d, m), F8)],
        compiler_params=_params(("parallel",)),
        name="out_proj",
    )(sx.reshape(1), merged, w_out, h, g.reshape(1, d).astype(F32))


SUBLANES = 8


def _merge_sort_network(n):
    pairs = []
    p = 1
    while p < n:
        k = p
        while k >= 1:
            for j in range(k % p, n - k, 2 * k):
                for i in range(min(k, n - j - k)):
                    if (i + j) // (2 * p) == (i + j + k) // (2 * p):
                        pairs.append((i + j, i + j + k))
            k //= 2
        p *= 2
    return pairs


def _top_values(x, k):
    n = x.shape[0] // SUBLANES
    v = [x[SUBLANES * r:SUBLANES * (r + 1)] for r in range(n)]
    for i, j in _merge_sort_network(n):
        v[i], v[j] = jnp.maximum(v[i], v[j]), jnp.minimum(v[i], v[j])
    vals = []
    for t in range(k):
        m = jnp.max(v[0], axis=0, keepdims=True)
        vals.append(m)
        if t + 1 < k:
            hit = v[0] == m
            for d in range(min(k - t - 1, n)):
                v[d] = jnp.where(hit, v[d + 1] if d + 1 < n else NEG_INF, v[d])
    return vals


def _route_kernel(xt_ref, wqt_ref, keys_ref, sf_ref, sb_ref):
    tb = xt_ref.shape[1]
    qt = jnp.dot(wqt_ref[...], xt_ref[...], preferred_element_type=F32).astype(BF16)
    for h in range(PEER_HEADS):
        s = []
        for c in range(2):
            r0 = (h * 2 + c) * PEER_HALF
            s.append(jnp.dot(keys_ref[h, c], qt[r0:r0 + PEER_HALF, :], preferred_element_type=F32))
        a = _top_values(s[0], PEER_TOPK + 1)
        bv = _top_values(s[1], PEER_TOPK + 1)
        b16 = jnp.concatenate(bv[0:PEER_TOPK], axis=0)
        b8 = b16[0:8]
        tail = jnp.concatenate([a[0] + bv[PEER_TOPK]] + [a[k] + bv[0] for k in range(PEER_TOPK - 2, PEER_TOPK + 1)]
                               + [jnp.full((4, tb), NEG_INF, F32)], axis=0)
        cand = jnp.concatenate([a[0] + b16] + [a[k] + b8 for k in range(1, PEER_TOPK - 2)] + [tail], axis=0)
        best = _top_values(cand, PEER_TOPK + 1)
        zsum = jnp.ones_like(best[0])
        for r in range(1, PEER_TOPK):
            zsum = zsum + jnp.exp(best[r] - best[0])
        cut = 0.5 * (best[PEER_TOPK - 1] + best[PEER_TOPK])
        need = cut - s[0]
        count = jnp.zeros_like(need)
        rank2 = jnp.ones_like(need)
        for r in range(PEER_TOPK):
            count = jnp.where(bv[r] >= need, float(r + 1), count)
            rank2 = jnp.where(bv[r] > s[1], float(r + 2), rank2)
        sf_ref[h, 0] = count
        sf_ref[h, 1] = jnp.exp(s[0] - a[0]) / zsum
        sb_ref[h, 0] = rank2.astype(sb_ref.dtype)
        sb_ref[h, 1] = jnp.exp(s[1] - bv[0]).astype(sb_ref.dtype)


def _route(xt, wq_t, keys):
    d, m = xt.shape
    tb = _pick(m, (256, 128))
    return pl.pallas_call(
        _route_kernel,
        grid=(m // tb,),
        in_specs=[pl.BlockSpec((d, tb), lambda i: (0, i)),
                  pl.BlockSpec(wq_t.shape, lambda i: (0, 0)),
                  pl.BlockSpec(keys.shape, lambda i: (0, 0, 0, 0))],
        out_specs=[pl.BlockSpec((PEER_HEADS, 2, PEER_NKEYS, tb), lambda i: (0, 0, 0, i)),
                   pl.BlockSpec((PEER_HEADS, 2, PEER_NKEYS, tb), lambda i: (0, 0, 0, i))],
        out_shape=[jax.ShapeDtypeStruct((PEER_HEADS, 2, PEER_NKEYS, m), F32),
                   jax.ShapeDtypeStruct((PEER_HEADS, 2, PEER_NKEYS, m), BF16)],
        compiler_params=_params(("parallel",)),
        name="peer_route",
    )(xt, wq_t, keys)


PEER_STEP = 1024
PEER_JBLK = 32


def _peer_kernel(inv_ref, xt_ref, u_ref, vt_ref, sf_ref, sb_ref, o_ref, *, tb):
    e = pl.program_id(1)

    @pl.when(e == 0)
    def _():
        o_ref[...] = jnp.zeros_like(o_ref)

    n_i = PEER_STEP // PEER_NKEYS
    i0 = pl.multiple_of(e * n_i, n_i)
    st = jnp.dot(u_ref[...], xt_ref[...], preferred_element_type=F32)
    inv = inv_ref[0]
    jh = PEER_JBLK
    wdt = sb_ref.dtype
    cols = []
    for lb in range(tb // LANES):
        ls = slice(lb * LANES, (lb + 1) * LANES)
        cnt = [sf_ref[h, 0, pl.ds(i0, n_i), ls] for h in range(PEER_HEADS)]
        e1g = [sf_ref[h, 1, pl.ds(i0, n_i), ls] for h in range(PEER_HEADS)]
        rows = [[(jnp.broadcast_to(cnt[h][ii:ii + 1, :], (jh, LANES)).astype(wdt),
                  jnp.broadcast_to(e1g[h][ii:ii + 1, :], (jh, LANES)).astype(wdt))
                 for ii in range(n_i)] for h in range(PEER_HEADS)]
        blocks = [[None] * (PEER_NKEYS // jh) for _ in range(n_i)]
        for jb in range(PEER_NKEYS // jh):
            js = slice(jb * jh, (jb + 1) * jh)
            w = [jnp.zeros((jh, LANES), wdt) for _ in range(n_i)]
            for h in range(PEER_HEADS):
                rk = sb_ref[h, 0, js, ls]
                e2 = sb_ref[h, 1, js, ls]
                for ii in range(n_i):
                    n_b, e1_b = rows[h][ii]
                    w[ii] = w[ii] + jnp.where(rk <= n_b, e2 * e1_b, jnp.zeros_like(e2))
            for ii in range(n_i):
                sc = st[ii * PEER_NKEYS + jb * jh:ii * PEER_NKEYS + (jb + 1) * jh, ls] * inv
                act = 0.5 * sc * (1.0 + lax.erf(sc * (2.0 ** -0.5)))
                blocks[ii][jb] = (w[ii] * act.astype(wdt)).astype(BF16)
        cols.append(jnp.concatenate([blk for row in blocks for blk in row], axis=0))
    a = jnp.concatenate(cols, axis=1)
    o_ref[...] += jnp.dot(vt_ref[...], a, preferred_element_type=F32)


def _peer(inv, xt, u, vt, sf, sb):
    d, m = xt.shape
    ne = u.shape[0]
    tb = _pick(m, (512, 256, 128))
    route_spec = pl.BlockSpec((PEER_HEADS, 2, PEER_NKEYS, tb), lambda i, e: (0, 0, 0, i))
    return pl.pallas_call(
        functools.partial(_peer_kernel, tb=tb),
        grid=(m // tb, ne // PEER_STEP),
        in_specs=[pl.BlockSpec(memory_space=pltpu.SMEM),
                  pl.BlockSpec((d, tb), lambda i, e: (0, i)),
                  pl.BlockSpec((PEER_STEP, d), lambda i, e: (e, 0)),
                  pl.BlockSpec((d, PEER_STEP), lambda i, e: (0, e)),
                  route_spec, route_spec],
        out_specs=pl.BlockSpec((d, tb), lambda i, e: (0, i)),
        out_shape=jax.ShapeDtypeStruct((d, m), F32),
        compiler_params=_params(("parallel", "arbitrary")),
        name="peer_experts",
    )(inv.reshape(1), xt, u, vt, sf, sb)


def _final_kernel(h_ref, pt_ref, g_ref, o_ref):
    h = h_ref[0] + pt_ref[...].T
    ms = jnp.mean(h * h, axis=-1, keepdims=True)
    o_ref[0] = h * lax.rsqrt(ms + NORM_EPS) * g_ref[...]


def _final(h2, peer_t, g, b, lp, seq):
    d = h2.shape[-1]
    nc = lp // CHUNK
    skip = (LEAD + N_META) // CHUNK
    return pl.pallas_call(
        _final_kernel,
        grid=(b, seq // CHUNK),
        in_specs=[pl.BlockSpec((1, CHUNK, d), lambda i, c: (i, c + skip, 0)),
                  pl.BlockSpec((d, CHUNK), lambda i, c: (0, i * nc + c + skip)),
                  pl.BlockSpec((1, d), lambda i, c: (0, 0))],
        out_specs=pl.BlockSpec((1, CHUNK, d), lambda i, c: (i, c, 0)),
        out_shape=jax.ShapeDtypeStruct((b, seq, d), F32),
        compiler_params=_params(("parallel", "parallel")),
        name="final_norm",
    )(h2.reshape(b, lp, d), peer_t, g.reshape(1, d).astype(F32))


def kernel(x, meta_tokens, norm_mix_g, w_in, conv_w, conv_b, dt_bias, a_log, d_skip, ssm_norm_g,
           w_ret_o, w_ssm_o, w_out, norm_ffn_g, peer_w_q, peer_sub_keys, peer_u, peer_v, norm_final_g):
    b, seq, d = x.shape
    assert seq % CHUNK == 0 and (LEAD + N_META) % CHUNK == 0
    lp = LEAD + N_META + seq
    m = b * lp
    d_qk = d
    d_v = w_ret_o.shape[1]
    d_in = ssm_norm_g.shape[-1]
    gn = SSM_GROUPS * SSM_STATE
    n_heads = d_in // SSM_HEAD_DIM
    dk = d_qk // RET_HEADS

    h0 = jnp.concatenate([jnp.zeros((b, LEAD, d), x.dtype),
                          jnp.broadcast_to(meta_tokens.astype(x.dtype)[None], (b, N_META, d)), x], axis=1)
    h0 = h0.reshape(m, d)

    w = w_in[0]
    o_q, o_k, o_v, o_g, o_z, o_xbc = 0, d_qk, 2 * d_qk, 2 * d_qk + d_v, 2 * d_qk + 2 * d_v, 2 * d_qk + 2 * d_v + d_in
    o_dt = o_xbc + d_in + 2 * gn
    o_gate = o_dt + n_heads
    perm = np.concatenate([hh * dk + np.concatenate([np.arange(0, dk, 2), np.arange(1, dk, 2)])
                           for hh in range(RET_HEADS)])
    w_qk = jnp.concatenate([w[:, o_q:o_k][:, perm], w[:, o_k:o_v][:, perm]], axis=1).astype(BF16)
    w_vg = w[:, o_v:o_z].astype(BF16)
    w_z = w[:, o_z:o_xbc].astype(BF16)
    w_xbc = w[:, o_xbc:o_dt].astype(BF16)
    w_dt = jnp.pad(w[:, o_dt:o_gate], ((0, 0), (0, LANES - n_heads))).astype(BF16)
    w_gate = w[:, o_gate:].astype(BF16)

    pos = jnp.maximum(jnp.arange(lp, dtype=F32) - LEAD, 0.0)
    inv = ROPE_BASE ** (-jnp.arange(dk // 2, dtype=F32) / (dk // 2))
    ang = pos[:, None] * inv[None, :]
    cos, sin = jnp.cos(ang), jnp.sin(ang)
    log_gamma = jnp.log(1.0 - 2.0 ** (-5.0 - jnp.arange(RET_HEADS, dtype=F32)))

    n = _rmsnorm(h0, norm_mix_g[0])
    p_qk = _matmul(n, w_qk, BF16, "proj_qk")
    p_vg = _matmul(n, w_vg, BF16, "proj_vg")
    p_z = _matmul(n, w_z, BF16, "proj_z")
    p_xbc = _proj_conv(n, w_xbc, conv_w[0], conv_b[0], lp)
    p_gate = _matmul(n, w_gate, BF16, "proj_gate")
    dt_raw = _matmul(n, w_dt, F32, "proj_dt")

    y_ret = _retention(p_qk, p_vg, cos, sin, log_gamma, b, lp)
    y_ssm = _ssd(p_z, p_xbc, dt_raw, dt_bias[0], a_log[0], d_skip[0], ssm_norm_g[0], b, lp)
    merged = _merge(y_ret, y_ssm, w_ret_o[0].astype(BF16), w_ssm_o[0].astype(BF16), p_gate)
    sx = _pow2_scale(math.sqrt(d) * jnp.max(jnp.abs(norm_ffn_g[0])))
    h2, xt, xt8 = _out_proj(merged, w_out[0].astype(BF16), h0, norm_ffn_g[0], sx)

    sf, sb = _route(xt, peer_w_q[0].T.astype(BF16), peer_sub_keys[0].astype(BF16))
    su = _pow2_scale(jnp.max(jnp.abs(peer_u[0])))
    u8 = (peer_u[0] * su).astype(F8)
    peer_t = _peer(1.0 / (su * sx), xt8, u8, peer_v[0].T.astype(BF16), sf, sb)
    return _final(h2, peer_t, norm_final_g, b, lp, seq)
```

```python
import functools
import math

import jax
import jax.numpy as jnp
import numpy as np
from jax import lax
from jax.experimental import pallas as pl
from jax.experimental.pallas import tpu as pltpu

N_META = 16
CHUNK = 128
LEAD = (-N_META) % CHUNK
NORM_EPS = 1e-6
ROPE_BASE = 10000.0

RET_HEADS = 8
SSM_HEAD_DIM = 64
SSM_GROUPS = 8
SSM_STATE = 128
CONV_WIDTH = 4
CONV_TAIL = 16

PEER_HEADS = 8
PEER_NKEYS = 128
PEER_TOPK = 16
PEER_HALF = 128

LANES = 128
VMEM_LIMIT = 56 * 1024 * 1024

F32 = jnp.float32
BF16 = jnp.bfloat16
F8 = jnp.float8_e4m3fn
F8_SLACK = 1.25
NEG_INF = float("-inf")


def _pick(n, prefs):
    for p in prefs:
        if n % p == 0:
            return p
    raise ValueError(f"no tile for {n} in {prefs}")


def _params(sem):
    return pltpu.CompilerParams(dimension_semantics=sem, vmem_limit_bytes=VMEM_LIMIT)


def _silu(v):
    return v * jax.nn.sigmoid(v)


def _rmsnorm_kernel(h_ref, g_ref, o_ref):
    h = h_ref[...]
    ms = jnp.mean(h * h, axis=-1, keepdims=True)
    o_ref[...] = (h * lax.rsqrt(ms + NORM_EPS) * g_ref[...]).astype(o_ref.dtype)


def _rmsnorm(h, g):
    m, d = h.shape
    tm = _pick(m, (512, 256, 128))
    return pl.pallas_call(
        _rmsnorm_kernel,
        grid=(m // tm,),
        in_specs=[pl.BlockSpec((tm, d), lambda i: (i, 0)),
                  pl.BlockSpec((1, d), lambda i: (0, 0))],
        out_specs=pl.BlockSpec((tm, d), lambda i: (i, 0)),
        out_shape=jax.ShapeDtypeStruct((m, d), BF16),
        compiler_params=_params(("parallel",)),
        name="rmsnorm",
    )(h, g.reshape(1, d).astype(F32))


def _mm_kernel(a_ref, w_ref, o_ref):
    o_ref[...] = jnp.dot(a_ref[...], w_ref[...], preferred_element_type=F32).astype(o_ref.dtype)


def _matmul(a, w, out_dtype, name):
    m, k = a.shape
    n = w.shape[1]
    tm = _pick(m, (1280, 640, 512, 384, 256, 128))
    tn = _pick(n, (1024, 512, 256, 128))
    return pl.pallas_call(
        _mm_kernel,
        grid=(m // tm, n // tn),
        in_specs=[pl.BlockSpec((tm, k), lambda i, j: (i, 0)),
                  pl.BlockSpec((k, tn), lambda i, j: (0, j))],
        out_specs=pl.BlockSpec((tm, tn), lambda i, j: (i, j)),
        out_shape=jax.ShapeDtypeStruct((m, n), out_dtype),
        compiler_params=_params(("parallel", "parallel")),
        name=name,
    )(a, w)


def _ret_kernel(lg_ref, q_ref, k_ref, v_ref, g_ref, cos_ref, sin_ref, o_ref, state_ref, *, dk, dv):
    c = pl.program_id(1)

    @pl.when(c == 0)
    def _():
        state_ref[...] = jnp.zeros_like(state_ref)

    cos = cos_ref[...]
    sin = sin_ref[...]
    row = lax.broadcasted_iota(jnp.int32, (CHUNK, CHUNK), 0).astype(F32)
    col = lax.broadcasted_iota(jnp.int32, (CHUNK, CHUNK), 1).astype(F32)
    rel = row - col
    half = dk // 2

    def rot(t):
        t1 = t[:, :half]
        t2 = t[:, half:]
        return jnp.concatenate([t1 * cos - t2 * sin, t1 * sin + t2 * cos], axis=1)

    for h in range(RET_HEADS):
        lg = lg_ref[h]
        dmask = jnp.where(rel >= 0, jnp.exp(lg * jnp.maximum(rel, 0.0)), 0.0)
        xi = jnp.exp(lg * (row + 1.0))
        zeta = jnp.exp(lg * (CHUNK - 1.0 - row))
        decay = jnp.exp(jnp.full((1, dv), lg * CHUNK, F32))

        qr = rot(q_ref[0, :, h * dk:(h + 1) * dk].astype(F32))
        kr = rot(k_ref[0, :, h * dk:(h + 1) * dk].astype(F32)) * (dk ** -0.5)
        v = v_ref[0, :, h * dv:(h + 1) * dv]
        qb = qr.astype(BF16)
        s = lax.dot_general(qb, kr.astype(BF16), (((1,), (1,)), ((), ())),
                            preferred_element_type=F32) * dmask
        inner = jnp.dot(s.astype(BF16), v, preferred_element_type=F32)
        st = state_ref[h]
        cross = jnp.dot(qb, st.astype(BF16), preferred_element_type=F32) * jnp.tile(xi, (1, dv // CHUNK))
        kz = (kr * jnp.tile(zeta, (1, dk // CHUNK))).astype(BF16)
        state_ref[h] = st * decay + lax.dot_general(kz, v, (((0,), (0,)), ((), ())),
                                                    preferred_element_type=F32)
        o = inner + cross
        ms = jnp.mean(o * o, axis=-1, keepdims=True)
        on = o * lax.rsqrt(ms + NORM_EPS)
        g = g_ref[0, :, h * dv:(h + 1) * dv].astype(F32)
        o_ref[0, :, h * dv:(h + 1) * dv] = (_silu(g) * on).astype(o_ref.dtype)


def _retention(p_qk, p_vg, cos, sin, log_gamma, b, lp):
    d_qk = p_qk.shape[-1] // 2
    d_v = p_vg.shape[-1] // 2
    dk = d_qk // RET_HEADS
    dv = d_v // RET_HEADS
    nc = lp // CHUNK
    p_qk = p_qk.reshape(b, lp, 2 * d_qk)
    p_vg = p_vg.reshape(b, lp, 2 * d_v)
    out = pl.pallas_call(
        functools.partial(_ret_kernel, dk=dk, dv=dv),
        grid=(b, nc),
        in_specs=[pl.BlockSpec(memory_space=pltpu.SMEM),
                  pl.BlockSpec((1, CHUNK, d_qk), lambda i, c: (i, c, 0)),
                  pl.BlockSpec((1, CHUNK, d_qk), lambda i, c: (i, c, 1)),
                  pl.BlockSpec((1, CHUNK, d_v), lambda i, c: (i, c, 0)),
                  pl.BlockSpec((1, CHUNK, d_v), lambda i, c: (i, c, 1)),
                  pl.BlockSpec((CHUNK, dk // 2), lambda i, c: (c, 0)),
                  pl.BlockSpec((CHUNK, dk // 2), lambda i, c: (c, 0))],
        out_specs=pl.BlockSpec((1, CHUNK, d_v), lambda i, c: (i, c, 0)),
        out_shape=jax.ShapeDtypeStruct((b, lp, d_v), BF16),
        scratch_shapes=[pltpu.VMEM((RET_HEADS, dk, dv), F32)],
        compiler_params=_params(("parallel", "arbitrary")),
        name="retention",
    )(log_gamma, p_qk, p_qk, p_vg, p_vg, cos, sin)
    return out.reshape(b * lp, d_v)


PROJ_CONV_COLS = 256


def _proj_conv_kernel(a_ref, ap_ref, w_ref, cw_ref, cb_ref, o_ref, *, tiles_per_row):
    i = pl.program_id(0)
    tm, tn = o_ref.shape
    t_in_row = lax.rem(i, tiles_per_row)
    tc = PROJ_CONV_COLS
    row = t_in_row * tm + lax.broadcasted_iota(jnp.int32, (tm, tc), 0)

    def project(c):
        cs = slice(c * tc, (c + 1) * tc)
        x = jnp.dot(a_ref[...], w_ref[:, cs], preferred_element_type=F32)
        xp = jnp.dot(ap_ref[...], w_ref[:, cs], preferred_element_type=F32)
        return jnp.concatenate([jnp.where(t_in_row == 0, 0.0, xp), x], axis=0)

    def finish(c, ext):
        cs = slice(c * tc, (c + 1) * tc)
        acc = jnp.broadcast_to(cb_ref[:, cs], (tm, tc))
        for w in range(CONV_WIDTH):
            back = CONV_WIDTH - 1 - w
            shifted = ext if back == 0 else pltpu.roll(ext, back, axis=0)
            acc = acc + cw_ref[w:w + 1, cs] * shifted[CONV_TAIL:CONV_TAIL + tm, :]
        o_ref[:, cs] = jnp.where(row >= LEAD, _silu(acc), 0.0).astype(o_ref.dtype)

    n_c = tn // tc
    ext = project(0)
    for c in range(n_c):
        nxt = project(c + 1) if c + 1 < n_c else None
        finish(c, ext)
        ext = nxt


def _proj_conv(a, w, conv_w, conv_b, lp):
    m, k = a.shape
    n = w.shape[1]
    tm = _pick(lp, (640, 512, 384, 256, 128))
    tn = _pick(n, (1024, 512, 256, 128))
    sub = tm // CONV_TAIL
    return pl.pallas_call(
        functools.partial(_proj_conv_kernel, tiles_per_row=lp // tm),
        grid=(m // tm, n // tn),
        in_specs=[pl.BlockSpec((tm, k), lambda i, j: (i, 0)),
                  pl.BlockSpec((CONV_TAIL, k), lambda i, j: (jnp.maximum(i * sub - 1, 0), 0)),
                  pl.BlockSpec((k, tn), lambda i, j: (0, j)),
                  pl.BlockSpec((CONV_WIDTH, tn), lambda i, j: (0, j)),
                  pl.BlockSpec((1, tn), lambda i, j: (0, j))],
        out_specs=pl.BlockSpec((tm, tn), lambda i, j: (i, j)),
        out_shape=jax.ShapeDtypeStruct((m, n), BF16),
        compiler_params=_params(("parallel", "parallel")),
        name="proj_conv",
    )(a, a, w, conv_w.astype(F32), conv_b.reshape(1, n).astype(F32))


def _ssd_kernel(z_ref, x_ref, bm_ref, cm_ref, dt_ref, dtb_ref, alog_ref, dskip_ref,
                ng_ref, exp_ref, o_ref, state_ref, *, d_in):
    c = pl.program_id(1)
    gw = d_in // SSM_GROUPS
    hg = gw // SSM_HEAD_DIM

    @pl.when(c == 0)
    def _():
        state_ref[...] = jnp.zeros_like(state_ref)

    row_i = lax.broadcasted_iota(jnp.int32, (CHUNK, LANES), 0)
    col_i = lax.broadcasted_iota(jnp.int32, (CHUNK, LANES), 1)
    valid = (c * CHUNK + row_i) >= LEAD
    tri = row_i >= col_i

    dtr = dt_ref[0] + dtb_ref[...]
    dt = jnp.maximum(dtr, 0.0) + jnp.log1p(jnp.exp(-jnp.abs(dtr)))
    dt = jnp.where(valid, dt, 0.0)
    a = dt * (-jnp.exp(alog_ref[...]))
    ltri = jnp.where(tri, 1.0, 0.0)
    acs = jnp.dot(ltri, a, preferred_element_type=F32, precision=lax.Precision.HIGHEST)
    acs_t = acs.T
    acs_last = acs[CHUNK - 1:CHUNK, :]
    eacs = jnp.exp(acs)
    dtdec = dt * jnp.exp(acs_last - acs)
    stack = jnp.concatenate([dt, eacs, dtdec], axis=0).astype(BF16)

    lane_lo = col_i < SSM_HEAD_DIM
    for g in range(SSM_GROUPS):
        gs = slice(g * gw, (g + 1) * gw)
        ex = jnp.dot(stack, exp_ref[:, gs], preferred_element_type=F32)
        dt_x = ex[0:CHUNK]
        eacs_x = ex[CHUNK:2 * CHUNK]
        dtdec_x = ex[2 * CHUNK:3 * CHUNK]
        xs = x_ref[0, :, gs].astype(F32)
        bmat = bm_ref[0, :, g * SSM_STATE:(g + 1) * SSM_STATE]
        cmat = cm_ref[0, :, g * SSM_STATE:(g + 1) * SSM_STATE]
        xdt = xs * dt_x
        cb = lax.dot_general(cmat, bmat, (((1,), (1,)), ((), ())), preferred_element_type=F32)
        tiles = []
        for t in range(gw // LANES):
            xt = xdt[:, t * LANES:(t + 1) * LANES]
            acc = None
            for s in range(LANES // SSM_HEAD_DIM):
                hh = g * hg + t * (LANES // SSM_HEAD_DIM) + s
                diff = acs[:, hh:hh + 1] - acs_t[hh:hh + 1, :]
                gm = (cb * jnp.where(tri, jnp.exp(diff), 0.0)).astype(BF16)
                xh = jnp.where(lane_lo if s == 0 else jnp.logical_not(lane_lo), xt, 0.0).astype(BF16)
                part = jnp.dot(gm, xh, preferred_element_type=F32)
                acc = part if acc is None else acc + part
            tiles.append(acc)
        y_diag = jnp.concatenate(tiles, axis=1)
        st = state_ref[g]
        y_off = jnp.dot(cmat, st.astype(BF16), preferred_element_type=F32) * eacs_x
        xw = (xs * dtdec_x).astype(BF16)
        state_ref[g] = st * eacs_x[CHUNK - 1:CHUNK, :] + lax.dot_general(
            bmat, xw, (((0,), (0,)), ((), ())), preferred_element_type=F32)
        y = y_diag + y_off + dskip_ref[:, gs] * xs
        y = y * _silu(z_ref[0, :, gs].astype(F32))
        ms = jnp.mean(y * y, axis=-1, keepdims=True)
        o_ref[0, :, gs] = (y * lax.rsqrt(ms + NORM_EPS) * ng_ref[:, gs]).astype(o_ref.dtype)


def _ssd(p_z, p_xbc, dt_raw, dt_bias, a_log, d_skip, norm_g, b, lp):
    d_in = norm_g.shape[-1]
    gn = SSM_GROUPS * SSM_STATE
    n_heads = d_in // SSM_HEAD_DIM
    nc = lp // CHUNK
    p_z = p_z.reshape(b, lp, d_in)
    p_xbc = p_xbc.reshape(b, lp, d_in + 2 * gn)
    dt_raw = dt_raw.reshape(b, lp, LANES)
    pad = LANES - n_heads
    dtb = jnp.pad(dt_bias.astype(F32), (0, pad)).reshape(1, LANES)
    alog = jnp.pad(a_log.astype(F32), (0, pad)).reshape(1, LANES)
    dskip = jnp.repeat(d_skip.astype(F32), SSM_HEAD_DIM).reshape(1, d_in)
    expand = (jnp.arange(LANES)[:, None] == (jnp.arange(d_in)[None, :] // SSM_HEAD_DIM)).astype(BF16)
    full = lambda shape: pl.BlockSpec(shape, lambda i, c: (0,) * len(shape))
    out = pl.pallas_call(
        functools.partial(_ssd_kernel, d_in=d_in),
        grid=(b, nc),
        in_specs=[pl.BlockSpec((1, CHUNK, d_in), lambda i, c: (i, c, 0)),
                  pl.BlockSpec((1, CHUNK, d_in), lambda i, c: (i, c, 0)),
                  pl.BlockSpec((1, CHUNK, gn), lambda i, c: (i, c, d_in // gn)),
                  pl.BlockSpec((1, CHUNK, gn), lambda i, c: (i, c, d_in // gn + 1)),
                  pl.BlockSpec((1, CHUNK, LANES), lambda i, c: (i, c, 0)),
                  full((1, LANES)), full((1, LANES)),
                  full((1, d_in)), full((1, d_in)), full((LANES, d_in))],
        out_specs=pl.BlockSpec((1, CHUNK, d_in), lambda i, c: (i, c, 0)),
        out_shape=jax.ShapeDtypeStruct((b, lp, d_in), BF16),
        scratch_shapes=[pltpu.VMEM((SSM_GROUPS, SSM_STATE, d_in // SSM_GROUPS), F32)],
        compiler_params=_params(("parallel", "arbitrary")),
        name="ssd",
    )(p_z, p_xbc, p_xbc, p_xbc, dt_raw, dtb, alog, dskip, norm_g.reshape(1, d_in).astype(F32), expand)
    return out.reshape(b * lp, d_in)


def _merge_kernel(yr_ref, ys_ref, wr_ref, ws_ref, gr_ref, gs_ref, o_ref):
    yr = jnp.dot(yr_ref[...], wr_ref[...], preferred_element_type=F32)
    ys = jnp.dot(ys_ref[...], ws_ref[...], preferred_element_type=F32)
    o_ref[...] = (jax.nn.sigmoid(gr_ref[...].astype(F32)) * yr
                  + jax.nn.sigmoid(gs_ref[...].astype(F32)) * ys).astype(o_ref.dtype)


def _merge(y_ret, y_ssm, w_ret_o, w_ssm_o, p_gate):
    m, kr = y_ret.shape
    ks = y_ssm.shape[1]
    d = w_ret_o.shape[1]
    tm = _pick(m, (640, 512, 384, 256, 128))
    tn = _pick(d, (512, 256, 128))
    nj = d // tn
    return pl.pallas_call(
        _merge_kernel,
        grid=(m // tm, nj),
        in_specs=[pl.BlockSpec((tm, kr), lambda i, j: (i, 0)),
                  pl.BlockSpec((tm, ks), lambda i, j: (i, 0)),
                  pl.BlockSpec((kr, tn), lambda i, j: (0, j)),
                  pl.BlockSpec((ks, tn), lambda i, j: (0, j)),
                  pl.BlockSpec((tm, tn), lambda i, j: (i, j)),
                  pl.BlockSpec((tm, tn), lambda i, j: (i, nj + j))],
        out_specs=pl.BlockSpec((tm, tn), lambda i, j: (i, j)),
        out_shape=jax.ShapeDtypeStruct((m, d), BF16),
        compiler_params=_params(("parallel", "parallel")),
        name="merge",
    )(y_ret, y_ssm, w_ret_o, w_ssm_o, p_gate, p_gate)


def _out_kernel(sx_ref, m_ref, w_ref, h_ref, g_ref, h2_ref, xt_ref, xt8_ref):
    h2 = h_ref[...] + jnp.dot(m_ref[...], w_ref[...], preferred_element_type=F32)
    h2_ref[...] = h2
    ms = jnp.mean(h2 * h2, axis=-1, keepdims=True)
    xn_t = (h2 * lax.rsqrt(ms + NORM_EPS) * g_ref[...]).T
    xt_ref[...] = xn_t.astype(xt_ref.dtype)
    xt8_ref[...] = (xn_t * sx_ref[0]).astype(xt8_ref.dtype)


def _pow2_scale(bound):
    bound = jnp.maximum(bound.astype(F32), jnp.finfo(F32).tiny)
    return jnp.exp2(jnp.floor(jnp.log2(float(jnp.finfo(F8).max) / bound)))


def _out_proj(merged, w_out, h, g, sx):
    m, d = h.shape
    tm = _pick(m, (256, 128))
    return pl.pallas_call(
        _out_kernel,
        grid=(m // tm,),
        in_specs=[pl.BlockSpec(memory_space=pltpu.SMEM),
                  pl.BlockSpec((tm, d), lambda i: (i, 0)),
                  pl.BlockSpec((d, d), lambda i: (0, 0)),
                  pl.BlockSpec((tm, d), lambda i: (i, 0)),
                  pl.BlockSpec((1, d), lambda i: (0, 0))],
        out_specs=[pl.BlockSpec((tm, d), lambda i: (i, 0)),
                   pl.BlockSpec((d, tm), lambda i: (0, i)),
                   pl.BlockSpec((d, tm), lambda i: (0, i))],
        out_shape=[jax.ShapeDtypeStruct((m, d), F32),
                   jax.ShapeDtypeStruct((d, m), BF16),
                   jax.ShapeDtypeStruct((d, m), F8)],
        compiler_params=_params(("parallel",)),
        name="out_proj",
    )(sx.reshape(1), merged, w_out, h, g.reshape(1, d).astype(F32))


SUBLANES = 8


def _merge_sort_network(n):
    pairs = []
    p = 1
    while p < n:
        k = p
        while k >= 1:
            for j in range(k % p, n - k, 2 * k):
                for i in range(min(k, n - j - k)):
                    if (i + j) // (2 * p) == (i + j + k) // (2 * p):
                        pairs.append((i + j, i + j + k))
            k //= 2
        p *= 2
    return pairs


def _top_values(x, k):
    n = x.shape[0] // SUBLANES
    v = [x[SUBLANES * r:SUBLANES * (r + 1)] for r in range(n)]
    for i, j in _merge_sort_network(n):
        v[i], v[j] = jnp.maximum(v[i], v[j]), jnp.minimum(v[i], v[j])
    vals = []
    for t in range(k):
        m = jnp.max(v[0], axis=0, keepdims=True)
        vals.append(m)
        if t + 1 < k:
            hit = v[0] == m
            for d in range(min(k - t - 1, n)):
                v[d] = jnp.where(hit, v[d + 1] if d + 1 < n else NEG_INF, v[d])
    return vals


def _route_kernel(xt_ref, wqt_ref, keys_ref, sf_ref, sb_ref):
    tb = xt_ref.shape[1]
    qt = jnp.dot(wqt_ref[...], xt_ref[...], preferred_element_type=F32).astype(BF16)
    for h in range(PEER_HEADS):
        s = []
        for c in range(2):
            r0 = (h * 2 + c) * PEER_HALF
            s.append(jnp.dot(keys_ref[h, c], qt[r0:r0 + PEER_HALF, :], preferred_element_type=F32))
        a = _top_values(s[0], PEER_TOPK + 1)
        bv = _top_values(s[1], PEER_TOPK + 1)
        b16 = jnp.concatenate(bv[0:PEER_TOPK], axis=0)
        b8 = b16[0:8]
        tail = jnp.concatenate([a[0] + bv[PEER_TOPK]] + [a[k] + bv[0] for k in range(PEER_TOPK - 2, PEER_TOPK + 1)]
                               + [jnp.full((4, tb), NEG_INF, F32)], axis=0)
        cand = jnp.concatenate([a[0] + b16] + [a[k] + b8 for k in range(1, PEER_TOPK - 2)] + [tail], axis=0)
        best = _top_values(cand, PEER_TOPK + 1)
        zsum = jnp.ones_like(best[0])
        for r in range(1, PEER_TOPK):
            zsum = zsum + jnp.exp(best[r] - best[0])
        cut = 0.5 * (best[PEER_TOPK - 1] + best[PEER_TOPK])
        need = cut - s[0]
        count = jnp.zeros_like(need)
        rank2 = jnp.ones_like(need)
        for r in range(PEER_TOPK):
            count = jnp.where(bv[r] >= need, float(r + 1), count)
            rank2 = jnp.where(bv[r] > s[1], float(r + 2), rank2)
        sf_ref[h, 0] = count
        sf_ref[h, 1] = jnp.exp(s[0] - a[0]) / zsum
        sb_ref[h, 0] = rank2.astype(sb_ref.dtype)
        sb_ref[h, 1] = jnp.exp(s[1] - bv[0]).astype(sb_ref.dtype)


def _route(xt, wq_t, keys):
    d, m = xt.shape
    tb = _pick(m, (256, 128))
    return pl.pallas_call(
        _route_kernel,
        grid=(m // tb,),
        in_specs=[pl.BlockSpec((d, tb), lambda i: (0, i)),
                  pl.BlockSpec(wq_t.shape, lambda i: (0, 0)),
                  pl.BlockSpec(keys.shape, lambda i: (0, 0, 0, 0))],
        out_specs=[pl.BlockSpec((PEER_HEADS, 2, PEER_NKEYS, tb), lambda i: (0, 0, 0, i)),
                   pl.BlockSpec((PEER_HEADS, 2, PEER_NKEYS, tb), lambda i: (0, 0, 0, i))],
        out_shape=[jax.ShapeDtypeStruct((PEER_HEADS, 2, PEER_NKEYS, m), F32),
                   jax.ShapeDtypeStruct((PEER_HEADS, 2, PEER_NKEYS, m), BF16)],
        compiler_params=_params(("parallel",)),
        name="peer_route",
    )(xt, wq_t, keys)


PEER_STEP = 1024
PEER_JBLK = 32


def _peer_kernel(inv_ref, xt_ref, u_ref, vt_ref, sf_ref, sb_ref, o_ref, *, tb):
    e = pl.program_id(1)

    @pl.when(e == 0)
    def _():
        o_ref[...] = jnp.zeros_like(o_ref)

    n_i = PEER_STEP // PEER_NKEYS
    i0 = pl.multiple_of(e * n_i, n_i)
    st = jnp.dot(u_ref[...], xt_ref[...], preferred_element_type=F32)
    inv = inv_ref[0]
    half_sa = 0.5 * inv_ref[1]
    jh = PEER_JBLK
    wdt = sb_ref.dtype
    cols = []
    for lb in range(tb // LANES):
        ls = slice(lb * LANES, (lb + 1) * LANES)
        cnt = [sf_ref[h, 0, pl.ds(i0, n_i), ls] for h in range(PEER_HEADS)]
        e1g = [sf_ref[h, 1, pl.ds(i0, n_i), ls] for h in range(PEER_HEADS)]
        rows = [[(jnp.broadcast_to(cnt[h][ii:ii + 1, :], (jh, LANES)).astype(wdt),
                  jnp.broadcast_to(e1g[h][ii:ii + 1, :], (jh, LANES)).astype(wdt))
                 for ii in range(n_i)] for h in range(PEER_HEADS)]
        blocks = [[None] * (PEER_NKEYS // jh) for _ in range(n_i)]
        for jb in range(PEER_NKEYS // jh):
            js = slice(jb * jh, (jb + 1) * jh)
            w = [jnp.zeros((jh, LANES), wdt) for _ in range(n_i)]
            for h in range(PEER_HEADS):
                rk = sb_ref[h, 0, js, ls]
                e2 = sb_ref[h, 1, js, ls]
                for ii in range(n_i):
                    n_b, e1_b = rows[h][ii]
                    w[ii] = w[ii] + jnp.where(rk <= n_b, e2 * e1_b, jnp.zeros_like(e2))
            for ii in range(n_i):
                sc = st[ii * PEER_NKEYS + jb * jh:ii * PEER_NKEYS + (jb + 1) * jh, ls] * inv
                act = half_sa * sc * (1.0 + lax.erf(sc * (2.0 ** -0.5)))
                blocks[ii][jb] = (w[ii] * act.astype(wdt)).astype(F8)
        cols.append(jnp.concatenate([blk for row in blocks for blk in row], axis=0))
    a = jnp.concatenate(cols, axis=1)
    o_ref[...] += jnp.dot(vt_ref[...], a, preferred_element_type=F32)


def _peer(inv, xt, u, vt, sf, sb):
    d, m = xt.shape
    ne = u.shape[0]
    tb = _pick(m, (512, 256, 128))
    route_spec = pl.BlockSpec((PEER_HEADS, 2, PEER_NKEYS, tb), lambda i, e: (0, 0, 0, i))
    return pl.pallas_call(
        functools.partial(_peer_kernel, tb=tb),
        grid=(m // tb, ne // PEER_STEP),
        in_specs=[pl.BlockSpec(memory_space=pltpu.SMEM),
                  pl.BlockSpec((d, tb), lambda i, e: (0, i)),
                  pl.BlockSpec((PEER_STEP, d), lambda i, e: (e, 0)),
                  pl.BlockSpec((d, PEER_STEP), lambda i, e: (0, e)),
                  route_spec, route_spec],
        out_specs=pl.BlockSpec((d, tb), lambda i, e: (0, i)),
        out_shape=jax.ShapeDtypeStruct((d, m), F32),
        compiler_params=_params(("parallel", "arbitrary")),
        name="peer_experts",
    )(inv, xt, u, vt, sf, sb)


def _final_kernel(ps_ref, h_ref, pt_ref, g_ref, o_ref):
    h = h_ref[0] + pt_ref[...].T * ps_ref[0]
    ms = jnp.mean(h * h, axis=-1, keepdims=True)
    o_ref[0] = h * lax.rsqrt(ms + NORM_EPS) * g_ref[...]


def _final(h2, peer_t, g, peer_scale, b, lp, seq):
    d = h2.shape[-1]
    nc = lp // CHUNK
    skip = (LEAD + N_META) // CHUNK
    return pl.pallas_call(
        _final_kernel,
        grid=(b, seq // CHUNK),
        in_specs=[pl.BlockSpec(memory_space=pltpu.SMEM),
                  pl.BlockSpec((1, CHUNK, d), lambda i, c: (i, c + skip, 0)),
                  pl.BlockSpec((d, CHUNK), lambda i, c: (0, i * nc + c + skip)),
                  pl.BlockSpec((1, d), lambda i, c: (0, 0))],
        out_specs=pl.BlockSpec((1, CHUNK, d), lambda i, c: (i, c, 0)),
        out_shape=jax.ShapeDtypeStruct((b, seq, d), F32),
        compiler_params=_params(("parallel", "parallel")),
        name="final_norm",
    )(peer_scale.reshape(1), h2.reshape(b, lp, d), peer_t, g.reshape(1, d).astype(F32))


def kernel(x, meta_tokens, norm_mix_g, w_in, conv_w, conv_b, dt_bias, a_log, d_skip, ssm_norm_g,
           w_ret_o, w_ssm_o, w_out, norm_ffn_g, peer_w_q, peer_sub_keys, peer_u, peer_v, norm_final_g):
    b, seq, d = x.shape
    assert seq % CHUNK == 0 and (LEAD + N_META) % CHUNK == 0
    lp = LEAD + N_META + seq
    m = b * lp
    d_qk = d
    d_v = w_ret_o.shape[1]
    d_in = ssm_norm_g.shape[-1]
    gn = SSM_GROUPS * SSM_STATE
    n_heads = d_in // SSM_HEAD_DIM
    dk = d_qk // RET_HEADS

    h0 = jnp.concatenate([jnp.zeros((b, LEAD, d), x.dtype),
                          jnp.broadcast_to(meta_tokens.astype(x.dtype)[None], (b, N_META, d)), x], axis=1)
    h0 = h0.reshape(m, d)

    w = w_in[0]
    o_q, o_k, o_v, o_g, o_z, o_xbc = 0, d_qk, 2 * d_qk, 2 * d_qk + d_v, 2 * d_qk + 2 * d_v, 2 * d_qk + 2 * d_v + d_in
    o_dt = o_xbc + d_in + 2 * gn
    o_gate = o_dt + n_heads
    perm = np.concatenate([hh * dk + np.concatenate([np.arange(0, dk, 2), np.arange(1, dk, 2)])
                           for hh in range(RET_HEADS)])
    w_qk = jnp.concatenate([w[:, o_q:o_k][:, perm], w[:, o_k:o_v][:, perm]], axis=1).astype(BF16)
    w_vg = w[:, o_v:o_z].astype(BF16)
    w_z = w[:, o_z:o_xbc].astype(BF16)
    w_xbc = w[:, o_xbc:o_dt].astype(BF16)
    w_dt = jnp.pad(w[:, o_dt:o_gate], ((0, 0), (0, LANES - n_heads))).astype(BF16)
    w_gate = w[:, o_gate:].astype(BF16)

    pos = jnp.maximum(jnp.arange(lp, dtype=F32) - LEAD, 0.0)
    inv = ROPE_BASE ** (-jnp.arange(dk // 2, dtype=F32) / (dk // 2))
    ang = pos[:, None] * inv[None, :]
    cos, sin = jnp.cos(ang), jnp.sin(ang)
    log_gamma = jnp.log(1.0 - 2.0 ** (-5.0 - jnp.arange(RET_HEADS, dtype=F32)))

    n = _rmsnorm(h0, norm_mix_g[0])
    p_qk = _matmul(n, w_qk, BF16, "proj_qk")
    p_vg = _matmul(n, w_vg, BF16, "proj_vg")
    p_z = _matmul(n, w_z, BF16, "proj_z")
    p_xbc = _proj_conv(n, w_xbc, conv_w[0], conv_b[0], lp)
    p_gate = _matmul(n, w_gate, BF16, "proj_gate")
    dt_raw = _matmul(n, w_dt, F32, "proj_dt")

    y_ret = _retention(p_qk, p_vg, cos, sin, log_gamma, b, lp)
    y_ssm = _ssd(p_z, p_xbc, dt_raw, dt_bias[0], a_log[0], d_skip[0], ssm_norm_g[0], b, lp)
    merged = _merge(y_ret, y_ssm, w_ret_o[0].astype(BF16), w_ssm_o[0].astype(BF16), p_gate)
    sx = _pow2_scale(math.sqrt(d) * jnp.max(jnp.abs(norm_ffn_g[0])))
    h2, xt, xt8 = _out_proj(merged, w_out[0].astype(BF16), h0, norm_ffn_g[0], sx)

    sf, sb = _route(xt, peer_w_q[0].T.astype(BF16), peer_sub_keys[0].astype(BF16))
    su = _pow2_scale(jnp.max(jnp.abs(peer_u[0])))
    u8 = (peer_u[0] * su).astype(F8)
    x_norm = math.sqrt(d) * jnp.max(jnp.abs(norm_ffn_g[0]))
    u_norm = jnp.sqrt(jnp.max(jnp.sum(jnp.square(peer_u[0]), axis=-1)))
    sa = _pow2_scale(F8_SLACK * PEER_HEADS * u_norm * x_norm)
    sv = _pow2_scale(jnp.max(jnp.abs(peer_v[0])))
    vt8 = (peer_v[0].T * sv).astype(F8)
    peer_t = _peer(jnp.stack([1.0 / (su * sx), sa]), xt8, u8, vt8, sf, sb)
    return _final(h2, peer_t, norm_final_g, 1.0 / (sa * sv), b, lp, seq)
```

```python
import functools
import math

import jax
import jax.numpy as jnp
import numpy as np
from jax import lax
from jax.experimental import pallas as pl
from jax.experimental.pallas import tpu as pltpu

N_META = 16
CHUNK = 128
LEAD = (-N_META) % CHUNK
NORM_EPS = 1e-6
ROPE_BASE = 10000.0

RET_HEADS = 8
SSM_HEAD_DIM = 64
SSM_GROUPS = 8
SSM_STATE = 128
CONV_WIDTH = 4
CONV_TAIL = 16

PEER_HEADS = 8
PEER_NKEYS = 128
PEER_TOPK = 16
PEER_HALF = 128

LANES = 128
VMEM_LIMIT = 56 * 1024 * 1024

F32 = jnp.float32
BF16 = jnp.bfloat16
F8 = jnp.float8_e4m3fn
F8_SLACK = 1.25
NEG_INF = float("-inf")


def _pick(n, prefs):
    for p in prefs:
        if n % p == 0:
            return p
    raise ValueError(f"no tile for {n} in {prefs}")


def _params(sem):
    return pltpu.CompilerParams(dimension_semantics=sem, vmem_limit_bytes=VMEM_LIMIT)


def _silu(v):
    return v * jax.nn.sigmoid(v)


def _rmsnorm_kernel(h_ref, g_ref, o_ref):
    h = h_ref[...]
    ms = jnp.mean(h * h, axis=-1, keepdims=True)
    o_ref[...] = (h * lax.rsqrt(ms + NORM_EPS) * g_ref[...]).astype(o_ref.dtype)


def _rmsnorm(h, g):
    m, d = h.shape
    tm = _pick(m, (512, 256, 128))
    return pl.pallas_call(
        _rmsnorm_kernel,
        grid=(m // tm,),
        in_specs=[pl.BlockSpec((tm, d), lambda i: (i, 0)),
                  pl.BlockSpec((1, d), lambda i: (0, 0))],
        out_specs=pl.BlockSpec((tm, d), lambda i: (i, 0)),
        out_shape=jax.ShapeDtypeStruct((m, d), BF16),
        compiler_params=_params(("parallel",)),
        name="rmsnorm",
    )(h, g.reshape(1, d).astype(F32))


def _mm_kernel(a_ref, w_ref, o_ref):
    o_ref[...] = jnp.dot(a_ref[...], w_ref[...], preferred_element_type=F32).astype(o_ref.dtype)


def _matmul(a, w, out_dtype, name):
    m, k = a.shape
    n = w.shape[1]
    tm = _pick(m, (1280, 640, 512, 384, 256, 128))
    tn = _pick(n, (1024, 512, 256, 128))
    return pl.pallas_call(
        _mm_kernel,
        grid=(m // tm, n // tn),
        in_specs=[pl.BlockSpec((tm, k), lambda i, j: (i, 0)),
                  pl.BlockSpec((k, tn), lambda i, j: (0, j))],
        out_specs=pl.BlockSpec((tm, tn), lambda i, j: (i, j)),
        out_shape=jax.ShapeDtypeStruct((m, n), out_dtype),
        compiler_params=_params(("parallel", "parallel")),
        name=name,
    )(a, w)


def _ret_kernel(lg_ref, q_ref, k_ref, v_ref, g_ref, cos_ref, sin_ref, o_ref, state_ref, *, dk, dv):
    c = pl.program_id(1)

    @pl.when(c == 0)
    def _():
        state_ref[...] = jnp.zeros_like(state_ref)

    cos = cos_ref[...]
    sin = sin_ref[...]
    row = lax.broadcasted_iota(jnp.int32, (CHUNK, CHUNK), 0).astype(F32)
    col = lax.broadcasted_iota(jnp.int32, (CHUNK, CHUNK), 1).astype(F32)
    rel = row - col
    half = dk // 2

    def rot(t):
        t1 = t[:, :half]
        t2 = t[:, half:]
        return jnp.concatenate([t1 * cos - t2 * sin, t1 * sin + t2 * cos], axis=1)

    for h in range(RET_HEADS):
        lg = lg_ref[h]
        dmask = jnp.where(rel >= 0, jnp.exp(lg * jnp.maximum(rel, 0.0)), 0.0)
        xi = jnp.exp(lg * (row + 1.0))
        zeta = jnp.exp(lg * (CHUNK - 1.0 - row))
        decay = jnp.exp(jnp.full((1, dv), lg * CHUNK, F32))

        qr = rot(q_ref[0, :, h * dk:(h + 1) * dk].astype(F32))
        kr = rot(k_ref[0, :, h * dk:(h + 1) * dk].astype(F32)) * (dk ** -0.5)
        v = v_ref[0, :, h * dv:(h + 1) * dv]
        qb = qr.astype(BF16)
        s = lax.dot_general(qb, kr.astype(BF16), (((1,), (1,)), ((), ())),
                            preferred_element_type=F32) * dmask
        inner = jnp.dot(s.astype(BF16), v, preferred_element_type=F32)
        st = state_ref[h]
        cross = jnp.dot(qb, st.astype(BF16), preferred_element_type=F32) * jnp.tile(xi, (1, dv // CHUNK))
        kz = (kr * jnp.tile(zeta, (1, dk // CHUNK))).astype(BF16)
        state_ref[h] = st * decay + lax.dot_general(kz, v, (((0,), (0,)), ((), ())),
                                                    preferred_element_type=F32)
        o = inner + cross
        ms = jnp.mean(o * o, axis=-1, keepdims=True)
        on = o * lax.rsqrt(ms + NORM_EPS)
        g = g_ref[0, :, h * dv:(h + 1) * dv].astype(F32)
        o_ref[0, :, h * dv:(h + 1) * dv] = (_silu(g) * on).astype(o_ref.dtype)


def _retention(p_all, d_qk, d_v, cos, sin, log_gamma, b, lp):
    dk = d_qk // RET_HEADS
    dv = d_v // RET_HEADS
    nc = lp // CHUNK
    assert (2 * d_qk) % d_v == 0
    v0 = 2 * d_qk // d_v
    p_qk = p_vg = p_all.reshape(b, lp, p_all.shape[-1])
    out = pl.pallas_call(
        functools.partial(_ret_kernel, dk=dk, dv=dv),
        grid=(b, nc),
        in_specs=[pl.BlockSpec(memory_space=pltpu.SMEM),
                  pl.BlockSpec((1, CHUNK, d_qk), lambda i, c: (i, c, 0)),
                  pl.BlockSpec((1, CHUNK, d_qk), lambda i, c: (i, c, 1)),
                  pl.BlockSpec((1, CHUNK, d_v), lambda i, c: (i, c, v0)),
                  pl.BlockSpec((1, CHUNK, d_v), lambda i, c: (i, c, v0 + 1)),
                  pl.BlockSpec((CHUNK, dk // 2), lambda i, c: (c, 0)),
                  pl.BlockSpec((CHUNK, dk // 2), lambda i, c: (c, 0))],
        out_specs=pl.BlockSpec((1, CHUNK, d_v), lambda i, c: (i, c, 0)),
        out_shape=jax.ShapeDtypeStruct((b, lp, d_v), BF16),
        scratch_shapes=[pltpu.VMEM((RET_HEADS, dk, dv), F32)],
        compiler_params=_params(("parallel", "arbitrary")),
        name="retention",
    )(log_gamma, p_qk, p_qk, p_vg, p_vg, cos, sin)
    return out.reshape(b * lp, d_v)


PROJ_CONV_COLS = 256


def _proj_conv_kernel(a_ref, ap_ref, w_ref, cw_ref, cb_ref, o_ref, *, tiles_per_row):
    i = pl.program_id(0)
    tm, tn = o_ref.shape
    t_in_row = lax.rem(i, tiles_per_row)
    tc = PROJ_CONV_COLS
    row = t_in_row * tm + lax.broadcasted_iota(jnp.int32, (tm, tc), 0)

    def project(c):
        cs = slice(c * tc, (c + 1) * tc)
        x = jnp.dot(a_ref[...], w_ref[:, cs], preferred_element_type=F32)
        xp = jnp.dot(ap_ref[...], w_ref[:, cs], preferred_element_type=F32)
        return jnp.concatenate([jnp.where(t_in_row == 0, 0.0, xp), x], axis=0)

    def finish(c, ext):
        cs = slice(c * tc, (c + 1) * tc)
        acc = jnp.broadcast_to(cb_ref[:, cs], (tm, tc))
        for w in range(CONV_WIDTH):
            back = CONV_WIDTH - 1 - w
            shifted = ext if back == 0 else pltpu.roll(ext, back, axis=0)
            acc = acc + cw_ref[w:w + 1, cs] * shifted[CONV_TAIL:CONV_TAIL + tm, :]
        o_ref[:, cs] = jnp.where(row >= LEAD, _silu(acc), 0.0).astype(o_ref.dtype)

    n_c = tn // tc
    ext = project(0)
    for c in range(n_c):
        nxt = project(c + 1) if c + 1 < n_c else None
        finish(c, ext)
        ext = nxt


def _proj_conv(a, w, conv_w, conv_b, lp):
    m, k = a.shape
    n = w.shape[1]
    tm = _pick(lp, (640, 512, 384, 256, 128))
    tn = _pick(n, (1024, 512, 256, 128))
    sub = tm // CONV_TAIL
    return pl.pallas_call(
        functools.partial(_proj_conv_kernel, tiles_per_row=lp // tm),
        grid=(m // tm, n // tn),
        in_specs=[pl.BlockSpec((tm, k), lambda i, j: (i, 0)),
                  pl.BlockSpec((CONV_TAIL, k), lambda i, j: (jnp.maximum(i * sub - 1, 0), 0)),
                  pl.BlockSpec((k, tn), lambda i, j: (0, j)),
                  pl.BlockSpec((CONV_WIDTH, tn), lambda i, j: (0, j)),
                  pl.BlockSpec((1, tn), lambda i, j: (0, j))],
        out_specs=pl.BlockSpec((tm, tn), lambda i, j: (i, j)),
        out_shape=jax.ShapeDtypeStruct((m, n), BF16),
        compiler_params=_params(("parallel", "parallel")),
        name="proj_conv",
    )(a, a, w, conv_w.astype(F32), conv_b.reshape(1, n).astype(F32))


def _ssd_kernel(z_ref, x_ref, bm_ref, cm_ref, dt_ref, dtb_ref, alog_ref, dskip_ref,
                ng_ref, exp_ref, o_ref, state_ref, *, d_in):
    c = pl.program_id(1)
    gw = d_in // SSM_GROUPS
    hg = gw // SSM_HEAD_DIM

    @pl.when(c == 0)
    def _():
        state_ref[...] = jnp.zeros_like(state_ref)

    row_i = lax.broadcasted_iota(jnp.int32, (CHUNK, LANES), 0)
    col_i = lax.broadcasted_iota(jnp.int32, (CHUNK, LANES), 1)
    valid = (c * CHUNK + row_i) >= LEAD
    tri = row_i >= col_i

    dtr = dt_ref[0] + dtb_ref[...]
    dt = jnp.maximum(dtr, 0.0) + jnp.log1p(jnp.exp(-jnp.abs(dtr)))
    dt = jnp.where(valid, dt, 0.0)
    a = dt * (-jnp.exp(alog_ref[...]))
    ltri = jnp.where(tri, 1.0, 0.0)
    acs = jnp.dot(ltri, a, preferred_element_type=F32, precision=lax.Precision.HIGHEST)
    acs_t = acs.T
    acs_last = acs[CHUNK - 1:CHUNK, :]
    eacs = jnp.exp(acs)
    dtdec = dt * jnp.exp(acs_last - acs)
    stack = jnp.concatenate([dt, eacs, dtdec], axis=0).astype(BF16)

    lane_lo = col_i < SSM_HEAD_DIM
    for g in range(SSM_GROUPS):
        gs = slice(g * gw, (g + 1) * gw)
        ex = jnp.dot(stack, exp_ref[:, gs], preferred_element_type=F32)
        dt_x = ex[0:CHUNK]
        eacs_x = ex[CHUNK:2 * CHUNK]
        dtdec_x = ex[2 * CHUNK:3 * CHUNK]
        xs = x_ref[0, :, gs].astype(F32)
        bmat = bm_ref[0, :, g * SSM_STATE:(g + 1) * SSM_STATE]
        cmat = cm_ref[0, :, g * SSM_STATE:(g + 1) * SSM_STATE]
        xdt = xs * dt_x
        cb = lax.dot_general(cmat, bmat, (((1,), (1,)), ((), ())), preferred_element_type=F32)
        tiles = []
        for t in range(gw // LANES):
            xt = xdt[:, t * LANES:(t + 1) * LANES]
            acc = None
            for s in range(LANES // SSM_HEAD_DIM):
                hh = g * hg + t * (LANES // SSM_HEAD_DIM) + s
                diff = acs[:, hh:hh + 1] - acs_t[hh:hh + 1, :]
                gm = (cb * jnp.where(tri, jnp.exp(diff), 0.0)).astype(BF16)
                xh = jnp.where(lane_lo if s == 0 else jnp.logical_not(lane_lo), xt, 0.0).astype(BF16)
                part = jnp.dot(gm, xh, preferred_element_type=F32)
                acc = part if acc is None else acc + part
            tiles.append(acc)
        y_diag = jnp.concatenate(tiles, axis=1)
        st = state_ref[g]
        y_off = jnp.dot(cmat, st.astype(BF16), preferred_element_type=F32) * eacs_x
        xw = (xs * dtdec_x).astype(BF16)
        state_ref[g] = st * eacs_x[CHUNK - 1:CHUNK, :] + lax.dot_general(
            bmat, xw, (((0,), (0,)), ((), ())), preferred_element_type=F32)
        y = y_diag + y_off + dskip_ref[:, gs] * xs
        y = y * _silu(z_ref[0, :, gs].astype(F32))
        ms = jnp.mean(y * y, axis=-1, keepdims=True)
        o_ref[0, :, gs] = (y * lax.rsqrt(ms + NORM_EPS) * ng_ref[:, gs]).astype(o_ref.dtype)


def _ssd(p_all, z_col, p_xbc, dt_raw, dt_bias, a_log, d_skip, norm_g, b, lp):
    d_in = norm_g.shape[-1]
    gn = SSM_GROUPS * SSM_STATE
    n_heads = d_in // SSM_HEAD_DIM
    nc = lp // CHUNK
    assert z_col % d_in == 0
    z0 = z_col // d_in
    p_z = p_all.reshape(b, lp, p_all.shape[-1])
    p_xbc = p_xbc.reshape(b, lp, d_in + 2 * gn)
    dt_raw = dt_raw.reshape(b, lp, LANES)
    pad = LANES - n_heads
    dtb = jnp.pad(dt_bias.astype(F32), (0, pad)).reshape(1, LANES)
    alog = jnp.pad(a_log.astype(F32), (0, pad)).reshape(1, LANES)
    dskip = jnp.repeat(d_skip.astype(F32), SSM_HEAD_DIM).reshape(1, d_in)
    expand = (jnp.arange(LANES)[:, None] == (jnp.arange(d_in)[None, :] // SSM_HEAD_DIM)).astype(BF16)
    full = lambda shape: pl.BlockSpec(shape, lambda i, c: (0,) * len(shape))
    out = pl.pallas_call(
        functools.partial(_ssd_kernel, d_in=d_in),
        grid=(b, nc),
        in_specs=[pl.BlockSpec((1, CHUNK, d_in), lambda i, c: (i, c, z0)),
                  pl.BlockSpec((1, CHUNK, d_in), lambda i, c: (i, c, 0)),
                  pl.BlockSpec((1, CHUNK, gn), lambda i, c: (i, c, d_in // gn)),
                  pl.BlockSpec((1, CHUNK, gn), lambda i, c: (i, c, d_in // gn + 1)),
                  pl.BlockSpec((1, CHUNK, LANES), lambda i, c: (i, c, 0)),
                  full((1, LANES)), full((1, LANES)),
                  full((1, d_in)), full((1, d_in)), full((LANES, d_in))],
        out_specs=pl.BlockSpec((1, CHUNK, d_in), lambda i, c: (i, c, 0)),
        out_shape=jax.ShapeDtypeStruct((b, lp, d_in), BF16),
        scratch_shapes=[pltpu.VMEM((SSM_GROUPS, SSM_STATE, d_in // SSM_GROUPS), F32)],
        compiler_params=_params(("parallel", "arbitrary")),
        name="ssd",
    )(p_z, p_xbc, p_xbc, p_xbc, dt_raw, dtb, alog, dskip, norm_g.reshape(1, d_in).astype(F32), expand)
    return out.reshape(b * lp, d_in)


def _merge_kernel(yr_ref, ys_ref, wr_ref, ws_ref, gr_ref, gs_ref, o_ref):
    yr = jnp.dot(yr_ref[...], wr_ref[...], preferred_element_type=F32)
    ys = jnp.dot(ys_ref[...], ws_ref[...], preferred_element_type=F32)
    o_ref[...] = (jax.nn.sigmoid(gr_ref[...].astype(F32)) * yr
                  + jax.nn.sigmoid(gs_ref[...].astype(F32)) * ys).astype(o_ref.dtype)


def _merge(y_ret, y_ssm, w_ret_o, w_ssm_o, p_gate, gate_col):
    m, kr = y_ret.shape
    ks = y_ssm.shape[1]
    d = w_ret_o.shape[1]
    tm = _pick(m, (640, 512, 384, 256, 128))
    tn = _pick(d, (512, 256, 128))
    nj = d // tn
    assert gate_col % tn == 0
    g0 = gate_col // tn
    return pl.pallas_call(
        _merge_kernel,
        grid=(m // tm, nj),
        in_specs=[pl.BlockSpec((tm, kr), lambda i, j: (i, 0)),
                  pl.BlockSpec((tm, ks), lambda i, j: (i, 0)),
                  pl.BlockSpec((kr, tn), lambda i, j: (0, j)),
                  pl.BlockSpec((ks, tn), lambda i, j: (0, j)),
                  pl.BlockSpec((tm, tn), lambda i, j: (i, g0 + j)),
                  pl.BlockSpec((tm, tn), lambda i, j: (i, g0 + nj + j))],
        out_specs=pl.BlockSpec((tm, tn), lambda i, j: (i, j)),
        out_shape=jax.ShapeDtypeStruct((m, d), BF16),
        compiler_params=_params(("parallel", "parallel")),
        name="merge",
    )(y_ret, y_ssm, w_ret_o, w_ssm_o, p_gate, p_gate)


def _out_kernel(sx_ref, m_ref, w_ref, h_ref, g_ref, h2_ref, xt_ref, xt8_ref):
    h2 = h_ref[...] + jnp.dot(m_ref[...], w_ref[...], preferred_element_type=F32)
    h2_ref[...] = h2
    ms = jnp.mean(h2 * h2, axis=-1, keepdims=True)
    xn_t = (h2 * lax.rsqrt(ms + NORM_EPS) * g_ref[...]).T
    xt_ref[...] = xn_t.astype(xt_ref.dtype)
    xt8_ref[...] = (xn_t * sx_ref[0]).astype(xt8_ref.dtype)


def _pow2_scale(bound):
    bound = jnp.maximum(bound.astype(F32), jnp.finfo(F32).tiny)
    return jnp.exp2(jnp.floor(jnp.log2(float(jnp.finfo(F8).max) / bound)))


def _out_proj(merged, w_out, h, g, sx):
    m, d = h.shape
    tm = _pick(m, (256, 128))
    return pl.pallas_call(
        _out_kernel,
        grid=(m // tm,),
        in_specs=[pl.BlockSpec(memory_space=pltpu.SMEM),
                  pl.BlockSpec((tm, d), lambda i: (i, 0)),
                  pl.BlockSpec((d, d), lambda i: (0, 0)),
                  pl.BlockSpec((tm, d), lambda i: (i, 0)),
                  pl.BlockSpec((1, d), lambda i: (0, 0))],
        out_specs=[pl.BlockSpec((tm, d), lambda i: (i, 0)),
                   pl.BlockSpec((d, tm), lambda i: (0, i)),
                   pl.BlockSpec((d, tm), lambda i: (0, i))],
        out_shape=[jax.ShapeDtypeStruct((m, d), F32),
                   jax.ShapeDtypeStruct((d, m), BF16),
                   jax.ShapeDtypeStruct((d, m), F8)],
        compiler_params=_params(("parallel",)),
        name="out_proj",
    )(sx.reshape(1), merged, w_out, h, g.reshape(1, d).astype(F32))


SUBLANES = 8


def _merge_sort_network(n):
    pairs = []
    p = 1
    while p < n:
        k = p
        while k >= 1:
            for j in range(k % p, n - k, 2 * k):
                for i in range(min(k, n - j - k)):
                    if (i + j) // (2 * p) == (i + j + k) // (2 * p):
                        pairs.append((i + j, i + j + k))
            k //= 2
        p *= 2
    return pairs


def _top_values(x, k):
    n = x.shape[0] // SUBLANES
    v = [x[SUBLANES * r:SUBLANES * (r + 1)] for r in range(n)]
    for i, j in _merge_sort_network(n):
        v[i], v[j] = jnp.maximum(v[i], v[j]), jnp.minimum(v[i], v[j])
    vals = []
    for t in range(k):
        m = jnp.max(v[0], axis=0, keepdims=True)
        vals.append(m)
        if t + 1 < k:
            hit = v[0] == m
            for d in range(min(k - t - 1, n)):
                v[d] = jnp.where(hit, v[d + 1] if d + 1 < n else NEG_INF, v[d])
    return vals


def _route_kernel(xt_ref, wqt_ref, keys_ref, sf_ref, sb_ref):
    tb = xt_ref.shape[1]
    qt = jnp.dot(wqt_ref[...], xt_ref[...], preferred_element_type=F32).astype(BF16)
    for h in range(PEER_HEADS):
        s = []
        for c in range(2):
            r0 = (h * 2 + c) * PEER_HALF
            s.append(jnp.dot(keys_ref[h, c], qt[r0:r0 + PEER_HALF, :], preferred_element_type=F32))
        a = _top_values(s[0], PEER_TOPK + 1)
        bv = _top_values(s[1], PEER_TOPK + 1)
        b16 = jnp.concatenate(bv[0:PEER_TOPK], axis=0)
        b8 = b16[0:8]
        tail = jnp.concatenate([a[0] + bv[PEER_TOPK]] + [a[k] + bv[0] for k in range(PEER_TOPK - 2, PEER_TOPK + 1)]
                               + [jnp.full((4, tb), NEG_INF, F32)], axis=0)
        cand = jnp.concatenate([a[0] + b16] + [a[k] + b8 for k in range(1, PEER_TOPK - 2)] + [tail], axis=0)
        best = _top_values(cand, PEER_TOPK + 1)
        zsum = jnp.ones_like(best[0])
        for r in range(1, PEER_TOPK):
            zsum = zsum + jnp.exp(best[r] - best[0])
        cut = 0.5 * (best[PEER_TOPK - 1] + best[PEER_TOPK])
        need = cut - s[0]
        count = jnp.zeros_like(need)
        rank2 = jnp.ones_like(need)
        for r in range(PEER_TOPK):
            count = jnp.where(bv[r] >= need, float(r + 1), count)
            rank2 = jnp.where(bv[r] > s[1], float(r + 2), rank2)
        sf_ref[h, 0] = count
        sf_ref[h, 1] = jnp.exp(s[0] - a[0]) / zsum
        sb_ref[h, 0] = rank2.astype(sb_ref.dtype)
        sb_ref[h, 1] = jnp.exp(s[1] - bv[0]).astype(sb_ref.dtype)


def _route(xt, wq_t, keys):
    d, m = xt.shape
    tb = _pick(m, (256, 128))
    return pl.pallas_call(
        _route_kernel,
        grid=(m // tb,),
        in_specs=[pl.BlockSpec((d, tb), lambda i: (0, i)),
                  pl.BlockSpec(wq_t.shape, lambda i: (0, 0)),
                  pl.BlockSpec(keys.shape, lambda i: (0, 0, 0, 0))],
        out_specs=[pl.BlockSpec((PEER_HEADS, 2, PEER_NKEYS, tb), lambda i: (0, 0, 0, i)),
                   pl.BlockSpec((PEER_HEADS, 2, PEER_NKEYS, tb), lambda i: (0, 0, 0, i))],
        out_shape=[jax.ShapeDtypeStruct((PEER_HEADS, 2, PEER_NKEYS, m), F32),
                   jax.ShapeDtypeStruct((PEER_HEADS, 2, PEER_NKEYS, m), BF16)],
        compiler_params=_params(("parallel",)),
        name="peer_route",
    )(xt, wq_t, keys)


PEER_STEP = 1024
PEER_JBLK = 32


def _peer_kernel(inv_ref, xt_ref, u_ref, vt_ref, sf_ref, sb_ref, o_ref, *, tb):
    e = pl.program_id(1)

    @pl.when(e == 0)
    def _():
        o_ref[...] = jnp.zeros_like(o_ref)

    n_i = PEER_STEP // PEER_NKEYS
    i0 = pl.multiple_of(e * n_i, n_i)
    st = jnp.dot(u_ref[...], xt_ref[...], preferred_element_type=F32)
    inv = inv_ref[0]
    half_sa = 0.5 * inv_ref[1]
    jh = PEER_JBLK
    wdt = sb_ref.dtype
    cols = []
    for lb in range(tb // LANES):
        ls = slice(lb * LANES, (lb + 1) * LANES)
        cnt = [sf_ref[h, 0, pl.ds(i0, n_i), ls] for h in range(PEER_HEADS)]
        e1g = [sf_ref[h, 1, pl.ds(i0, n_i), ls] for h in range(PEER_HEADS)]
        rows = [[(jnp.broadcast_to(cnt[h][ii:ii + 1, :], (jh, LANES)).astype(wdt),
                  jnp.broadcast_to(e1g[h][ii:ii + 1, :], (jh, LANES)).astype(wdt))
                 for ii in range(n_i)] for h in range(PEER_HEADS)]
        blocks = [[None] * (PEER_NKEYS // jh) for _ in range(n_i)]
        for jb in range(PEER_NKEYS // jh):
            js = slice(jb * jh, (jb + 1) * jh)
            w = [jnp.zeros((jh, LANES), wdt) for _ in range(n_i)]
            for h in range(PEER_HEADS):
                rk = sb_ref[h, 0, js, ls]
                e2 = sb_ref[h, 1, js, ls]
                for ii in range(n_i):
                    n_b, e1_b = rows[h][ii]
                    w[ii] = w[ii] + jnp.where(rk <= n_b, e2 * e1_b, jnp.zeros_like(e2))
            for ii in range(n_i):
                sc = st[ii * PEER_NKEYS + jb * jh:ii * PEER_NKEYS + (jb + 1) * jh, ls] * inv
                act = half_sa * sc * (1.0 + lax.erf(sc * (2.0 ** -0.5)))
                blocks[ii][jb] = (w[ii] * act.astype(wdt)).astype(F8)
        cols.append(jnp.concatenate([blk for row in blocks for blk in row], axis=0))
    a = jnp.concatenate(cols, axis=1)
    o_ref[...] += jnp.dot(vt_ref[...], a, preferred_element_type=F32)


def _peer(inv, xt, u, vt, sf, sb):
    d, m = xt.shape
    ne = u.shape[0]
    tb = _pick(m, (512, 256, 128))
    route_spec = pl.BlockSpec((PEER_HEADS, 2, PEER_NKEYS, tb), lambda i, e: (0, 0, 0, i))
    return pl.pallas_call(
        functools.partial(_peer_kernel, tb=tb),
        grid=(m // tb, ne // PEER_STEP),
        in_specs=[pl.BlockSpec(memory_space=pltpu.SMEM),
                  pl.BlockSpec((d, tb), lambda i, e: (0, i)),
                  pl.BlockSpec((PEER_STEP, d), lambda i, e: (e, 0)),
                  pl.BlockSpec((d, PEER_STEP), lambda i, e: (0, e)),
                  route_spec, route_spec],
        out_specs=pl.BlockSpec((d, tb), lambda i, e: (0, i)),
        out_shape=jax.ShapeDtypeStruct((d, m), F32),
        compiler_params=_params(("parallel", "arbitrary")),
        name="peer_experts",
    )(inv, xt, u, vt, sf, sb)


def _final_kernel(ps_ref, h_ref, pt_ref, g_ref, o_ref):
    h = h_ref[0] + pt_ref[...].T * ps_ref[0]
    ms = jnp.mean(h * h, axis=-1, keepdims=True)
    o_ref[0] = h * lax.rsqrt(ms + NORM_EPS) * g_ref[...]


def _final(h2, peer_t, g, peer_scale, b, lp, seq):
    d = h2.shape[-1]
    nc = lp // CHUNK
    skip = (LEAD + N_META) // CHUNK
    return pl.pallas_call(
        _final_kernel,
        grid=(b, seq // CHUNK),
        in_specs=[pl.BlockSpec(memory_space=pltpu.SMEM),
                  pl.BlockSpec((1, CHUNK, d), lambda i, c: (i, c + skip, 0)),
                  pl.BlockSpec((d, CHUNK), lambda i, c: (0, i * nc + c + skip)),
                  pl.BlockSpec((1, d), lambda i, c: (0, 0))],
        out_specs=pl.BlockSpec((1, CHUNK, d), lambda i, c: (i, c, 0)),
        out_shape=jax.ShapeDtypeStruct((b, seq, d), F32),
        compiler_params=_params(("parallel", "parallel")),
        name="final_norm",
    )(peer_scale.reshape(1), h2.reshape(b, lp, d), peer_t, g.reshape(1, d).astype(F32))


def kernel(x, meta_tokens, norm_mix_g, w_in, conv_w, conv_b, dt_bias, a_log, d_skip, ssm_norm_g,
           w_ret_o, w_ssm_o, w_out, norm_ffn_g, peer_w_q, peer_sub_keys, peer_u, peer_v, norm_final_g):
    b, seq, d = x.shape
    assert seq % CHUNK == 0 and (LEAD + N_META) % CHUNK == 0
    lp = LEAD + N_META + seq
    m = b * lp
    d_qk = d
    d_v = w_ret_o.shape[1]
    d_in = ssm_norm_g.shape[-1]
    gn = SSM_GROUPS * SSM_STATE
    n_heads = d_in // SSM_HEAD_DIM
    dk = d_qk // RET_HEADS

    h0 = jnp.concatenate([jnp.zeros((b, LEAD, d), x.dtype),
                          jnp.broadcast_to(meta_tokens.astype(x.dtype)[None], (b, N_META, d)), x], axis=1)
    h0 = h0.reshape(m, d)

    w = w_in[0]
    o_q, o_k, o_v, o_g, o_z, o_xbc = 0, d_qk, 2 * d_qk, 2 * d_qk + d_v, 2 * d_qk + 2 * d_v, 2 * d_qk + 2 * d_v + d_in
    o_dt = o_xbc + d_in + 2 * gn
    o_gate = o_dt + n_heads
    perm = np.concatenate([hh * dk + np.concatenate([np.arange(0, dk, 2), np.arange(1, dk, 2)])
                           for hh in range(RET_HEADS)])
    w_all = jnp.concatenate([w[:, o_q:o_k][:, perm], w[:, o_k:o_v][:, perm], w[:, o_v:o_xbc], w[:, o_gate:]],
                            axis=1).astype(BF16)
    z_col = o_z
    gate_col = o_xbc
    w_xbc = w[:, o_xbc:o_dt].astype(BF16)
    w_dt = jnp.pad(w[:, o_dt:o_gate], ((0, 0), (0, LANES - n_heads))).astype(BF16)

    pos = jnp.maximum(jnp.arange(lp, dtype=F32) - LEAD, 0.0)
    inv = ROPE_BASE ** (-jnp.arange(dk // 2, dtype=F32) / (dk // 2))
    ang = pos[:, None] * inv[None, :]
    cos, sin = jnp.cos(ang), jnp.sin(ang)
    log_gamma = jnp.log(1.0 - 2.0 ** (-5.0 - jnp.arange(RET_HEADS, dtype=F32)))

    n = _rmsnorm(h0, norm_mix_g[0])
    p_all = _matmul(n, w_all, BF16, "proj_all")
    p_xbc = _proj_conv(n, w_xbc, conv_w[0], conv_b[0], lp)
    dt_raw = _matmul(n, w_dt, F32, "proj_dt")

    y_ret = _retention(p_all, d_qk, d_v, cos, sin, log_gamma, b, lp)
    y_ssm = _ssd(p_all, z_col, p_xbc, dt_raw, dt_bias[0], a_log[0], d_skip[0], ssm_norm_g[0], b, lp)
    merged = _merge(y_ret, y_ssm, w_ret_o[0].astype(BF16), w_ssm_o[0].astype(BF16), p_all, gate_col)
    sx = _pow2_scale(math.sqrt(d) * jnp.max(jnp.abs(norm_ffn_g[0])))
    h2, xt, xt8 = _out_proj(merged, w_out[0].astype(BF16), h0, norm_ffn_g[0], sx)

    sf, sb = _route(xt, peer_w_q[0].T.astype(BF16), peer_sub_keys[0].astype(BF16))
    su = _pow2_scale(jnp.max(jnp.abs(peer_u[0])))
    u8 = (peer_u[0] * su).astype(F8)
    x_norm = math.sqrt(d) * jnp.max(jnp.abs(norm_ffn_g[0]))
    u_norm = jnp.sqrt(jnp.max(jnp.sum(jnp.square(peer_u[0]), axis=-1)))
    sa = _pow2_scale(F8_SLACK * PEER_HEADS * u_norm * x_norm)
    sv = _pow2_scale(jnp.max(jnp.abs(peer_v[0])))
    vt8 = (peer_v[0].T * sv).astype(F8)
    peer_t = _peer(jnp.stack([1.0 / (su * sx), sa]), xt8, u8, vt8, sf, sb)
    return _final(h2, peer_t, norm_final_g, 1.0 / (sa * sv), b, lp, seq)
```
